```python
import jax, jax.numpy as jnp
from jax import lax
import numpy as np

D_MODEL = 4096
BATCH = 8
SEQ = 2048
DEPTH = 1
DEC_BATCH = 32
DEC_SEQ = 16
PAST_LEN = 1024

CHUNK = 64
D_BRANCH = D_MODEL // 2
GMLP_CHUNK = 128
GMLP_GROUPS = 8
GMLP_GDIM = D_BRANCH // GMLP_GROUPS
N_HEADS = 16
HEAD_DIM = D_BRANCH // N_HEADS
PAST_CHUNKS = 8
BAND = (PAST_CHUNKS + 1) * CHUNK
WINDOW = PAST_CHUNKS * CHUNK
REL_CLIP = 128
PLE_DIM = 256
EPS = 1e-6
NEG_INF = -1e30

kernel_name = 'hybrid_gmlp_bandattn_stream_step'


def _rmsnorm(x, g):
    xf = x.astype(jnp.float32)
    y = xf * lax.rsqrt(jnp.mean(xf * xf, axis=-1, keepdims=True) + EPS)
    return (y * g.astype(jnp.float32)).astype(x.dtype)


def _layernorm(x, g, b):
    xf = x.astype(jnp.float32)
    xc = xf - jnp.mean(xf, axis=-1, keepdims=True)
    y = xc * lax.rsqrt(jnp.mean(xc * xc, axis=-1, keepdims=True) + EPS)
    return (y * g.astype(jnp.float32) + b.astype(jnp.float32)).astype(x.dtype)


def _chunk_causal_mask(n):
    c = jnp.arange(n) // CHUNK
    return c[None, :] <= c[:, None]


def _rel_bias(rel_bias, d):
    idx = jnp.clip(d, -REL_CLIP, REL_CLIP) + REL_CLIP
    return jnp.take(rel_bias.astype(jnp.float32), idx, axis=1)


def _project_in(x, pre_g, w_in):
    h = _rmsnorm(x, pre_g)
    sizes = [D_BRANCH] * 7 + [D_MODEL] * 2
    offs = [int(o) for o in np.cumsum(sizes)[:-1]]
    return jnp.split(h @ w_in, offs, axis=-1)


def _gmlp_uv(u, v, ln_g, ln_b):
    u = jax.nn.gelu(u, approximate=False)
    vn = _layernorm(jax.nn.gelu(v, approximate=False), ln_g, ln_b)
    return u, vn


def _sgu_prompt(vn, w_s, b_s):
    B, S, _ = vn.shape
    n = S // GMLP_CHUNK
    w = jnp.where(_chunk_causal_mask(GMLP_CHUNK)[None], w_s, 0)
    vg = vn.reshape(B, n, GMLP_CHUNK, GMLP_GROUPS, GMLP_GDIM)
    out = jnp.einsum('gij,bnjgc->bnigc', w, vg) + b_s.T[None, None, :, :, None]
    return out.reshape(B, S, D_BRANCH)


def _sgu_sample(vn, w_s, b_s):
    B, T, _ = vn.shape
    w = jnp.where(_chunk_causal_mask(T)[None], w_s[:, :T, :T], 0)
    vg = vn.reshape(B, T, GMLP_GROUPS, GMLP_GDIM)
    out = jnp.einsum('gij,bjgc->bigc', w, vg) + b_s[:, :T].T[None, :, :, None]
    return out.reshape(B, T, D_BRANCH)


def _band_attention_prompt(q, k, v, rel_bias):
    B, S, H, Dh = q.shape
    n_c = S // CHUNK
    qc = (q * (Dh ** -0.5)).reshape(B, n_c, CHUNK, H, Dh)
    pad = ((0, 0), (WINDOW, 0), (0, 0), (0, 0))
    kc = jnp.pad(k, pad).reshape(B, n_c + PAST_CHUNKS, CHUNK, H, Dh)
    vc = jnp.pad(v, pad).reshape(B, n_c + PAST_CHUNKS, CHUNK, H, Dh)
    scores = jnp.concatenate(
        [jnp.einsum('bcqhd,bckhd->bhcqk', qc, kc[:, s:s + n_c]) for s in range(PAST_CHUNKS + 1)],
        axis=-1).astype(jnp.float32)
    q_off = jnp.arange(CHUNK)
    k_off = jnp.arange(BAND) - WINDOW
    bias = _rel_bias(rel_bias, q_off[:, None] - k_off[None, :])
    valid = (jnp.arange(n_c)[:, None] + (jnp.arange(BAND) // CHUNK)[None, :] - PAST_CHUNKS) >= 0
    scores = jnp.where(valid[None, None, :, None, :], scores + bias[None, :, None], NEG_INF)
    probs = jax.nn.softmax(scores, axis=-1).astype(v.dtype)
    out = jnp.einsum('bhcqk,bckhd->bcqhd', probs[..., :CHUNK], vc[:, 0:n_c])
    for s in range(1, PAST_CHUNKS + 1):
        out = out + jnp.einsum('bhcqk,bckhd->bcqhd',
                               probs[..., s * CHUNK:(s + 1) * CHUNK], vc[:, s:s + n_c])
    return out.reshape(B, S, H, Dh)


def _band_attention_sample(q, k_new, v_new, k_cache, v_cache, rel_bias):
    Dh = q.shape[-1]
    T = q.shape[1]
    Lc = k_cache.shape[1]
    keys = jnp.concatenate([k_cache, k_new], axis=1)
    vals = jnp.concatenate([v_cache, v_new], axis=1)
    scores = jnp.einsum('bqhd,bkhd->bhqk', q * (Dh ** -0.5), keys).astype(jnp.float32)
    k_off = jnp.concatenate([jnp.arange(Lc) - Lc, jnp.arange(T)])
    bias = _rel_bias(rel_bias, jnp.arange(T)[:, None] - k_off[None, :])
    probs = jax.nn.softmax(scores + bias[None], axis=-1).astype(vals.dtype)
    return jnp.einsum('bhqk,bkhd->bqhd', probs, vals)


def _merge_and_residual(x, y_a, y_b, g_a, g_b, w_up_a, w_up_b, w_out, post_g):
    m = jax.nn.sigmoid(g_a) * (y_a @ w_up_a) + jax.nn.sigmoid(g_b) * (y_b @ w_up_b)
    return x + _rmsnorm(m @ w_out, post_g)


def _ple(x, p, w_pg, w_pp):
    return x + jax.nn.sigmoid(x @ w_pg) * (p @ w_pp)


def _prompt_layer(x, p, pre_g, post_g, w_in, ln_g, ln_b, w_s, b_s, rel_bias,
                  w_up_a, w_up_b, w_out, w_pg, w_pp):
    B, S, _ = x.shape
    u, v, z_a, q, k, val, z_b, g_a, g_b = _project_in(x, pre_g, w_in)
    u, vn = _gmlp_uv(u, v, ln_g, ln_b)
    y_a = u * _sgu_prompt(vn, w_s, b_s) * jax.nn.silu(z_a)
    qh = q.reshape(B, S, N_HEADS, HEAD_DIM)
    kh = k.reshape(B, S, N_HEADS, HEAD_DIM)
    vh = val.reshape(B, S, N_HEADS, HEAD_DIM)
    y_b = _band_attention_prompt(qh, kh, vh, rel_bias).reshape(B, S, D_BRANCH) * jax.nn.silu(z_b)
    x = _merge_and_residual(x, y_a, y_b, g_a, g_b, w_up_a, w_up_b, w_out, post_g)
    x = _ple(x, p, w_pg, w_pp)
    keep = min(WINDOW, S)
    return x, kh[:, S - keep:], vh[:, S - keep:], vn[:, S - GMLP_CHUNK:]


def _sample_layer(x, p, k_cache, v_cache, pre_g, post_g, w_in, ln_g, ln_b, w_s, b_s, rel_bias,
                  w_up_a, w_up_b, w_out, w_pg, w_pp):
    B, T, _ = x.shape
    u, v, z_a, q, k, val, z_b, g_a, g_b = _project_in(x, pre_g, w_in)
    u, vn = _gmlp_uv(u, v, ln_g, ln_b)
    y_a = u * _sgu_sample(vn, w_s, b_s) * jax.nn.silu(z_a)
    qh = q.reshape(B, T, N_HEADS, HEAD_DIM)
    kh = k.reshape(B, T, N_HEADS, HEAD_DIM)
    vh = val.reshape(B, T, N_HEADS, HEAD_DIM)
    y_b = _band_attention_sample(qh, kh, vh, k_cache, v_cache, rel_bias).reshape(B, T, D_BRANCH)
    y_b = y_b * jax.nn.silu(z_b)
    x = _merge_and_residual(x, y_a, y_b, g_a, g_b, w_up_a, w_up_b, w_out, post_g)
    x = _ple(x, p, w_pg, w_pp)
    return x, kh, vh, vn


def setup_inputs(seed: int = 0) -> dict:
    key = jax.random.key(seed)
    ks = jax.random.split(key, 20)

    def nrm(k, shape, scale):
        return jax.random.normal(k, shape, jnp.float32) * scale

    cache_len = min(WINDOW, PAST_LEN)
    d_in = 7 * D_BRANCH + 2 * D_MODEL
    return {
        'x_prompt': nrm(ks[0], (BATCH, SEQ, D_MODEL), 1.0),
        'x_sample': nrm(ks[1], (DEC_BATCH, DEC_SEQ, D_MODEL), 1.0),
        'cache_attn_k': nrm(ks[2], (DEPTH, DEC_BATCH, cache_len, N_HEADS, HEAD_DIM), 1.0),
        'cache_attn_v': nrm(ks[3], (DEPTH, DEC_BATCH, cache_len, N_HEADS, HEAD_DIM), 1.0),
        'p_prompt': nrm(ks[4], (DEPTH, BATCH, SEQ, PLE_DIM), 1.0),
        'p_sample': nrm(ks[5], (DEPTH, DEC_BATCH, DEC_SEQ, PLE_DIM), 1.0),
        'norm_pre_g': 1.0 + nrm(ks[6], (DEPTH, D_MODEL), 0.01),
        'norm_post_g': 1.0 + nrm(ks[7], (DEPTH, D_MODEL), 0.01),
        'w_in': nrm(ks[8], (DEPTH, D_MODEL, d_in), D_MODEL ** -0.5),
        'gmlp_ln_g': 1.0 + nrm(ks[9], (DEPTH, D_BRANCH), 0.01),
        'gmlp_ln_b': nrm(ks[10], (DEPTH, D_BRANCH), 0.01),
        'gmlp_w_s': nrm(ks[11], (DEPTH, GMLP_GROUPS, GMLP_CHUNK, GMLP_CHUNK), 0.5 * GMLP_CHUNK ** -0.5),
        'gmlp_b_s': 1.0 + nrm(ks[12], (DEPTH, GMLP_GROUPS, GMLP_CHUNK), 0.01),
        'attn_rel_bias': nrm(ks[13], (DEPTH, N_HEADS, 2 * REL_CLIP + 1), 0.1),
        'w_up_a': nrm(ks[14], (DEPTH, D_BRANCH, D_MODEL), D_BRANCH ** -0.5),
        'w_up_b': nrm(ks[15], (DEPTH, D_BRANCH, D_MODEL), D_BRANCH ** -0.5),
        'w_out': nrm(ks[16], (DEPTH, D_MODEL, D_MODEL), D_MODEL ** -0.5),
        'w_ple_gate': nrm(ks[17], (DEPTH, D_MODEL, D_MODEL), D_MODEL ** -0.5),
        'w_ple_proj': nrm(ks[18], (DEPTH, PLE_DIM, D_MODEL), PLE_DIM ** -0.5),
    }


def reference(x_prompt, x_sample, cache_attn_k, cache_attn_v, p_prompt, p_sample,
              norm_pre_g, norm_post_g, w_in, gmlp_ln_g, gmlp_ln_b, gmlp_w_s, gmlp_b_s,
              attn_rel_bias, w_up_a, w_up_b, w_out, w_ple_gate, w_ple_proj):
    xp = x_prompt
    xs = x_sample
    kp_l, vp_l, ks_l, vs_l, gp_l, gs_l = [], [], [], [], [], []
    for i in range(DEPTH):
        xp, kp, vp, gp = _prompt_layer(
            xp, p_prompt[i], norm_pre_g[i], norm_post_g[i], w_in[i], gmlp_ln_g[i], gmlp_ln_b[i],
            gmlp_w_s[i], gmlp_b_s[i], attn_rel_bias[i], w_up_a[i], w_up_b[i], w_out[i],
            w_ple_gate[i], w_ple_proj[i])
        xs, kn, vn, gn = _sample_layer(
            xs, p_sample[i], cache_attn_k[i], cache_attn_v[i], norm_pre_g[i], norm_post_g[i],
            w_in[i], gmlp_ln_g[i], gmlp_ln_b[i], gmlp_w_s[i], gmlp_b_s[i], attn_rel_bias[i],
            w_up_a[i], w_up_b[i], w_out[i], w_ple_gate[i], w_ple_proj[i])
        kp_l.append(kp)
        vp_l.append(vp)
        ks_l.append(kn)
        vs_l.append(vn)
        gp_l.append(gp)
        gs_l.append(gn)
    new_k_prompt = jnp.stack(kp_l)
    new_v_prompt = jnp.stack(vp_l)
    new_k_sample = jnp.stack(ks_l)
    new_v_sample = jnp.stack(vs_l)
    gmlp_v_prompt = jnp.stack(gp_l)
    gmlp_v_sample = jnp.stack(gs_l)
    return (xp, xs, new_k_prompt, new_v_prompt, new_k_sample, new_v_sample, gmlp_v_prompt, gmlp_v_sample)
```

```python
import functools
import math

import jax
import jax.numpy as jnp
from jax import lax
from jax.experimental import pallas as pl
from jax.experimental.pallas import tpu as pltpu

CHUNK = 64
GMLP_CHUNK = 128
GMLP_GROUPS = 8
N_HEADS = 16
PAST_CHUNKS = 8
WINDOW = PAST_CHUNKS * CHUNK
REL_CLIP = 128
EPS = 1e-6
NEG_INF = -1e30

LANES = 128
Q_BLOCK = 128
KEY_WIN = WINDOW + Q_BLOCK
VMEM_LIMIT_BYTES = 56 * 1024 * 1024

F32 = jnp.float32
BF16 = jnp.bfloat16


def _params(n_axes):
    return pltpu.CompilerParams(dimension_semantics=("arbitrary",) * n_axes,
                                vmem_limit_bytes=VMEM_LIMIT_BYTES)


def _tile(n, pref):
    if n <= pref:
        return n
    t = (pref // LANES) * LANES
    while n % t:
        t -= LANES
    return t


def _rmsnorm_kernel(x_ref, g_ref, o_ref):
    x = x_ref[...]
    ms = jnp.mean(x * x, axis=-1, keepdims=True)
    o_ref[...] = (x * lax.rsqrt(ms + EPS) * g_ref[...]).astype(o_ref.dtype)


def _rmsnorm_bf16(x, g):
    m, d = x.shape
    tr = _tile(m, 256)
    return pl.pallas_call(
        _rmsnorm_kernel,
        grid=(m // tr,),
        in_specs=[pl.BlockSpec((tr, d), lambda i: (i, 0)),
                  pl.BlockSpec((1, d), lambda i: (0, 0))],
        out_specs=pl.BlockSpec((tr, d), lambda i: (i, 0)),
        out_shape=jax.ShapeDtypeStruct((m, d), BF16),
        compiler_params=_params(1),
        name="pre_rmsnorm",
    )(x, g.reshape(1, d))


def _post_norm_kernel(o_ref, x_ref, g_ref, x1_ref, x1b_ref):
    o = o_ref[...]
    ms = jnp.mean(o * o, axis=-1, keepdims=True)
    x1 = x_ref[...] + o * lax.rsqrt(ms + EPS) * g_ref[...]
    x1_ref[...] = x1
    x1b_ref[...] = x1.astype(BF16)


def _post_norm_residual(o, x, g):
    m, d = x.shape
    tr = _tile(m, 256)
    row = pl.BlockSpec((tr, d), lambda i: (i, 0))
    return pl.pallas_call(
        _post_norm_kernel,
        grid=(m // tr,),
        in_specs=[row, row, pl.BlockSpec((1, d), lambda i: (0, 0))],
        out_specs=[row, row],
        out_shape=[jax.ShapeDtypeStruct((m, d), F32), jax.ShapeDtypeStruct((m, d), BF16)],
        compiler_params=_params(1),
        name="post_rmsnorm_residual",
    )(o, x, g.reshape(1, d))


def _proj_kernel(*refs, n_w, epilogue):
    h = refs[0][...]
    accs = [jnp.dot(h, w[...], preferred_element_type=F32) for w in refs[1:1 + n_w]]
    o_ref = refs[1 + n_w]
    o_ref[...] = epilogue(*accs).astype(o_ref.dtype)


def _col_map(i, j, *, first):
    return (0, first + j)


def _project(h, w, col_offsets, n_cols, epilogue, out_dtype, name, tn_pref=1024):
    m, k = h.shape
    tm = _tile(m, 1024)
    tn = _tile(math.gcd(n_cols, *col_offsets), tn_pref)
    n_w = len(col_offsets)
    w_specs = [pl.BlockSpec((k, tn), functools.partial(_col_map, first=off // tn))
               for off in col_offsets]
    return pl.pallas_call(
        functools.partial(_proj_kernel, n_w=n_w, epilogue=epilogue),
        grid=(m // tm, n_cols // tn),
        in_specs=[pl.BlockSpec((tm, k), lambda i, j: (i, 0))] + w_specs,
        out_specs=pl.BlockSpec((tm, tn), lambda i, j: (i, j)),
        out_shape=jax.ShapeDtypeStruct((m, n_cols), out_dtype),
        compiler_params=_params(2),
        name=name,
    )(h, *([w] * n_w))


def _gelu(x):
    return 0.5 * x * (1.0 + lax.erf(x * math.sqrt(0.5)))


def _gmlp_gate(u, z):
    return _gelu(u) * jax.nn.silu(z)


def _sgu_kernel(gv_ref, ug_ref, w_ref, bs_ref, lng_ref, lnb_ref, ya_ref, vn_ref, *, rows, gdim):
    x = gv_ref[...]
    xc = x - jnp.mean(x, axis=-1, keepdims=True)
    var = jnp.mean(xc * xc, axis=-1, keepdims=True)
    vn = xc * lax.rsqrt(var + EPS) * lng_ref[...] + lnb_ref[...]
    vn_ref[0] = vn
    kdim = w_ref.shape[2]
    vnb = vn.astype(BF16)
    if kdim > rows:
        vnb = jnp.concatenate([vnb, jnp.zeros((kdim - rows, vnb.shape[1]), BF16)], axis=0)
    i = lax.broadcasted_iota(jnp.int32, (rows, kdim), 0)
    j = lax.broadcasted_iota(jnp.int32, (rows, kdim), 1)
    visible = (j // CHUNK) <= (i // CHUNK)
    for g in range(w_ref.shape[0]):
        wg = jnp.where(visible, w_ref[g], 0.0).astype(BF16)
        sl = slice(g * gdim, (g + 1) * gdim)
        s = jnp.dot(wg, vnb[:, sl], preferred_element_type=F32) + bs_ref[:, g:g + 1]
        ya_ref[:, sl] = (ug_ref[:, sl].astype(F32) * s).astype(BF16)


def _sgu(gv, ug, w_s, b_s, ln_g, ln_b, batch, rows, keep_last_only):
    m, db = gv.shape
    groups = w_s.shape[0]
    n_chunks = m // (batch * rows)
    kdim = max(rows, LANES)
    w = w_s[:, :rows, :rows]
    if kdim > rows:
        w = jnp.pad(w, ((0, 0), (0, 0), (0, kdim - rows)))
    bs_t = b_s[:, :rows].T
    row_map = lambda b, c: (b * n_chunks + c, 0)
    full2 = lambda b, c: (0, 0)
    if keep_last_only:
        vn_shape, vn_spec = (batch, rows, db), pl.BlockSpec((1, rows, db), lambda b, c: (b, 0, 0))
    else:
        vn_shape = (batch * n_chunks, rows, db)
        vn_spec = pl.BlockSpec((1, rows, db), lambda b, c: (b * n_chunks + c, 0, 0))
    return pl.pallas_call(
        functools.partial(_sgu_kernel, rows=rows, gdim=db // groups),
        grid=(batch, n_chunks),
        in_specs=[pl.BlockSpec((rows, db), row_map),
                  pl.BlockSpec((rows, db), row_map),
                  pl.BlockSpec((groups, rows, kdim), lambda b, c: (0, 0, 0)),
                  pl.BlockSpec((rows, groups), full2),
                  pl.BlockSpec((1, db), full2),
                  pl.BlockSpec((1, db), full2)],
        out_specs=[pl.BlockSpec((rows, db), row_map), vn_spec],
        out_shape=[jax.ShapeDtypeStruct((m, db), BF16), jax.ShapeDtypeStruct(vn_shape, F32)],
        compiler_params=_params(2),
        name="sgu",
    )(gv, ug, w, bs_t, ln_g.reshape(1, db), ln_b.reshape(1, db))


def _bias_kernel(rb_ref, t_ref):
    wide = KEY_WIN + Q_BLOCK
    n_tab = rb_ref.shape[1]
    pos = lax.broadcasted_iota(jnp.int32, (n_tab, wide), 1)
    tab = lax.broadcasted_iota(jnp.int32, (n_tab, wide), 0)
    idx = jnp.clip(KEY_WIN - pos, -REL_CLIP, REL_CLIP) + REL_CLIP
    base = jnp.sum(jnp.where(tab == idx, rb_ref[0], 0.0), axis=0, keepdims=True)
    rolled = pltpu.roll(jnp.broadcast_to(base, (Q_BLOCK, wide)), 0, 1, stride=1, stride_axis=0)
    t = rolled[:, Q_BLOCK:]
    qc = lax.broadcasted_iota(jnp.int32, (Q_BLOCK, KEY_WIN), 0) // CHUNK
    kc = lax.broadcasted_iota(jnp.int32, (Q_BLOCK, KEY_WIN), 1) // CHUNK
    t_ref[0] = jnp.where((qc <= kc) & (kc <= qc + PAST_CHUNKS), t, NEG_INF)


def _bias_table(rel_bias):
    heads, n_rel = rel_bias.shape
    n_tab = -(-n_rel // 8) * 8
    rb = jnp.pad(rel_bias, ((0, 0), (0, n_tab - n_rel))).reshape(heads, n_tab, 1)
    return pl.pallas_call(
        _bias_kernel,
        grid=(heads,),
        in_specs=[pl.BlockSpec((1, n_tab, 1), lambda h: (h, 0, 0))],
        out_specs=pl.BlockSpec((1, Q_BLOCK, KEY_WIN), lambda h: (h, 0, 0)),
        out_shape=jax.ShapeDtypeStruct((heads, Q_BLOCK, KEY_WIN), F32),
        compiler_params=_params(1),
        name="attn_bias_table",
    )(rb)


def _softmax_pv(s, v):
    m = jnp.max(s, axis=-1, keepdims=True)
    p = jnp.exp(s - m)
    l = jnp.sum(p, axis=-1, keepdims=True)
    return jnp.dot(p.astype(BF16), v, preferred_element_type=F32) / l


_NT = (((1,), (1,)), ((), ()))


def _attn_prompt_kernel(q_ref, k_ref, v_ref, zb_ref, t_ref, y_ref, nk_ref, nv_ref, kb_s, vb_s,
                        *, seq, keep):
    kb_s[...] = k_ref[0].astype(BF16)
    vb_s[...] = v_ref[0].astype(BF16)
    nk_ref[0] = k_ref[0, seq - keep:, :]
    nv_ref[0] = v_ref[0, seq - keep:, :]

    def block(qs, ks, win, bias):
        rows = pl.ds(qs, Q_BLOCK)
        s = lax.dot_general(q_ref[0, rows, :], kb_s[pl.ds(ks, win), :], _NT,
                            preferred_element_type=F32) + bias
        o = _softmax_pv(s, vb_s[pl.ds(ks, win), :])
        y_ref[0, rows, :] = (o * zb_ref[0, rows, :].astype(F32)).astype(BF16)

    n_blocks = seq // Q_BLOCK
    n_head = min(WINDOW // Q_BLOCK, n_blocks)
    for r in range(n_head):
        block(r * Q_BLOCK, 0, (r + 1) * Q_BLOCK, t_ref[0, :, WINDOW - r * Q_BLOCK:])

    def body(r, carry):
        qs = pl.multiple_of(r * Q_BLOCK, Q_BLOCK)
        block(qs, pl.multiple_of(qs - WINDOW, Q_BLOCK), KEY_WIN, t_ref[0])
        return carry

    lax.fori_loop(n_head, n_blocks, body, 0)


def _attn_prompt(q, kv, zb, table, batch, seq):
    db = q.shape[1]
    dh = db // N_HEADS
    keep = min(WINDOW, seq)
    assert seq % Q_BLOCK == 0 and dh % LANES == 0
    shape3 = (batch, seq, db)
    kv3 = kv.reshape(batch, seq, 2 * db)
    head = pl.BlockSpec((1, seq, dh), lambda b, h: (b, 0, h))
    v_head = pl.BlockSpec((1, seq, dh), lambda b, h: (b, 0, N_HEADS + h))
    kept = pl.BlockSpec((1, keep, dh), lambda b, h: (b, 0, h))
    return pl.pallas_call(
        functools.partial(_attn_prompt_kernel, seq=seq, keep=keep),
        grid=(batch, N_HEADS),
        in_specs=[head, head, v_head, head,
                  pl.BlockSpec((1, Q_BLOCK, KEY_WIN), lambda b, h: (h, 0, 0))],
        out_specs=[head, kept, kept],
        out_shape=[jax.ShapeDtypeStruct(shape3, BF16),
                   jax.ShapeDtypeStruct((batch, keep, db), F32),
                   jax.ShapeDtypeStruct((batch, keep, db), F32)],
        scratch_shapes=[pltpu.VMEM((seq, dh), BF16), pltpu.VMEM((seq, dh), BF16)],
        compiler_params=_params(2),
        name="attn_prompt",
    )(q.reshape(shape3), kv3, kv3, zb.reshape(shape3), table)


def _attn_sample_kernel(q_ref, kn_ref, vn_ref, zb_ref, kc_ref, vc_ref, t_ref, y_ref, kcat_s, vcat_s,
                        *, n_new, n_cache, dh):
    n_keys = kcat_s.shape[0]
    n_real = n_cache + n_new
    pad = jnp.zeros((n_keys - n_real, dh), BF16)
    key_pos = lax.broadcasted_iota(jnp.int32, (n_new, n_keys), 1)
    for h in range(N_HEADS):
        sl = slice(h * dh, (h + 1) * dh)
        kcat_s[:n_cache, :] = kc_ref[0, :, sl].astype(BF16)
        kcat_s[n_cache:n_real, :] = kn_ref[0, :, sl].astype(BF16)
        kcat_s[n_real:, :] = pad
        vcat_s[:n_cache, :] = vc_ref[0, :, sl].astype(BF16)
        vcat_s[n_cache:n_real, :] = vn_ref[0, :, sl].astype(BF16)
        vcat_s[n_real:, :] = pad
        bias = jnp.where(key_pos < n_real, t_ref[h], NEG_INF)
        s = lax.dot_general(q_ref[0, :, sl], kcat_s[...], _NT, preferred_element_type=F32) + bias
        o = _softmax_pv(s, vcat_s[...])
        y_ref[0, :, sl] = (o * zb_ref[0, :, sl].astype(F32)).astype(BF16)


def _attn_sample(q, kv, zb, k_cache, v_cache, table, batch, n_new):
    db = q.shape[1]
    dh = db // N_HEADS
    n_cache = k_cache.shape[1]
    assert n_cache == WINDOW and n_new <= CHUNK and n_cache + n_new <= KEY_WIN
    n_keys = KEY_WIN
    new = pl.BlockSpec((1, n_new, db), lambda b: (b, 0, 0))
    new_v = pl.BlockSpec((1, n_new, db), lambda b: (b, 0, 1))
    cache = pl.BlockSpec((1, n_cache, db), lambda b: (b, 0, 0))
    shape3 = (batch, n_new, db)
    kv3 = kv.reshape(batch, n_new, 2 * db)
    y = pl.pallas_call(
        functools.partial(_attn_sample_kernel, n_new=n_new, n_cache=n_cache, dh=dh),
        grid=(batch,),
        in_specs=[new, new, new_v, new, cache, cache,
                  pl.BlockSpec((N_HEADS, n_new, n_keys), lambda b: (0, 0, 0))],
        out_specs=new,
        out_shape=jax.ShapeDtypeStruct(shape3, BF16),
        scratch_shapes=[pltpu.VMEM((n_keys, dh), BF16), pltpu.VMEM((n_keys, dh), BF16)],
        compiler_params=_params(1),
        name="attn_sample",
    )(q.reshape(shape3), kv3, kv3, zb.reshape(shape3), k_cache, v_cache, table[:, :n_new, :])
    return y.reshape(batch * n_new, db)


def _merge_kernel(ya_ref, yb_ref, wa_ref, wb_ref, ga_ref, gb_ref, m_ref):
    a = jnp.dot(ya_ref[...], wa_ref[...], preferred_element_type=F32)
    b = jnp.dot(yb_ref[...], wb_ref[...], preferred_element_type=F32)
    m_ref[...] = (ga_ref[...].astype(F32) * a + gb_ref[...].astype(F32) * b).astype(BF16)


def _merge(ya, yb, wa, wb, gates):
    m, kb = ya.shape
    n = wa.shape[1]
    tm, tn = _tile(m, 1024), _tile(n, 1024)
    lhs = pl.BlockSpec((tm, kb), lambda i, j: (i, 0))
    rhs = pl.BlockSpec((kb, tn), lambda i, j: (0, j))
    return pl.pallas_call(
        _merge_kernel,
        grid=(m // tm, n // tn),
        in_specs=[lhs, lhs, rhs, rhs,
                  pl.BlockSpec((tm, tn), lambda i, j: (i, j)),
                  pl.BlockSpec((tm, tn), functools.partial(lambda i, j, first: (i, first + j),
                                                           first=n // tn))],
        out_specs=pl.BlockSpec((tm, tn), lambda i, j: (i, j)),
        out_shape=jax.ShapeDtypeStruct((m, n), BF16),
        compiler_params=_params(2),
        name="merge",
    )(ya, yb, wa, wb, gates, gates)


def _ple_kernel(x1b_ref, wg_ref, p_ref, wp_ref, x1_ref, o_ref):
    gate = jax.nn.sigmoid(jnp.dot(x1b_ref[...], wg_ref[...], preferred_element_type=F32))
    proj = jnp.dot(p_ref[...].astype(BF16), wp_ref[...], preferred_element_type=F32)
    o_ref[...] = x1_ref[...] + gate * proj


def _ple(x1, x1b, p, w_pg, w_pp):
    m, d = x1.shape
    pd = p.shape[1]
    tm, tn = _tile(m, 1024), _tile(d, 512)
    tile = pl.BlockSpec((tm, tn), lambda i, j: (i, j))
    return pl.pallas_call(
        _ple_kernel,
        grid=(m // tm, d // tn),
        in_specs=[pl.BlockSpec((tm, d), lambda i, j: (i, 0)),
                  pl.BlockSpec((d, tn), lambda i, j: (0, j)),
                  pl.BlockSpec((tm, pd), lambda i, j: (i, 0)),
                  pl.BlockSpec((pd, tn), lambda i, j: (0, j)),
                  tile],
        out_specs=tile,
        out_shape=jax.ShapeDtypeStruct((m, d), F32),
        compiler_params=_params(2),
        name="ple",
    )(x1b, w_pg, p, w_pp, x1)


def _layer_common(x, w):
    d = x.shape[1]
    db = d // 2
    h = _rmsnorm_bf16(x, w["pre_g"])
    w_in = w["w_in"]
    ident = lambda a: a
    scale = (db // N_HEADS) ** -0.5
    ug = _project(h, w_in, (0, 2 * db), db, _gmlp_gate, BF16, "proj_gmlp_gate", tn_pref=512)
    gv = _project(h, w_in, (db,), db, _gelu, F32, "proj_gmlp_v")
    q = _project(h, w_in, (3 * db,), db, lambda a: a * scale, BF16, "proj_q")
    kv = _project(h, w_in, (4 * db,), 2 * db, ident, F32, "proj_kv")
    zb = _project(h, w_in, (6 * db,), db, jax.nn.silu, BF16, "proj_zb")
    gates = _project(h, w_in, (7 * db,), 2 * d, jax.nn.sigmoid, BF16, "proj_merge_gates")
    return ug, gv, q, kv, zb, gates


def _layer_tail(x, p, ya, yb, gates, w):
    m = _merge(ya, yb, w["w_up_a"], w["w_up_b"], gates)
    o = _project(m, w["w_out"], (0,), x.shape[1], lambda a: a, F32, "proj_out")
    x1, x1b = _post_norm_residual(o, x, w["post_g"])
    return _ple(x1, x1b, p, w["w_pg"], w["w_pp"])


def kernel(x_prompt, x_sample, cache_attn_k, cache_attn_v, p_prompt, p_sample, norm_pre_g, norm_post_g, w_in, gmlp_ln_g, gmlp_ln_b, gmlp_w_s, gmlp_b_s, attn_rel_bias, w_up_a, w_up_b, w_out, w_ple_gate, w_ple_proj):
    batch, seq, d = x_prompt.shape
    dec_batch, dec_seq, _ = x_sample.shape
    db = d // 2
    dh = db // N_HEADS
    depth = w_in.shape[0]
    assert seq % GMLP_CHUNK == 0 and dec_seq <= CHUNK

    xp = x_prompt.reshape(batch * seq, d)
    xs = x_sample.reshape(dec_batch * dec_seq, d)
    outs = [[] for _ in range(6)]
    for i in range(depth):
        w = dict(pre_g=norm_pre_g[i], post_g=norm_post_g[i], w_in=w_in[i].astype(BF16),
                 w_up_a=w_up_a[i].astype(BF16), w_up_b=w_up_b[i].astype(BF16),
                 w_out=w_out[i].astype(BF16), w_pg=w_ple_gate[i].astype(BF16),
                 w_pp=w_ple_proj[i].astype(BF16))
        table = _bias_table(attn_rel_bias[i])
        sgu_w = (gmlp_w_s[i], gmlp_b_s[i], gmlp_ln_g[i], gmlp_ln_b[i])

        ug, gv, q, kv, zb, gates = _layer_common(xp, w)
        ya, gp = _sgu(gv, ug, *sgu_w, batch=batch, rows=GMLP_CHUNK, keep_last_only=True)
        yb, kp, vp = _attn_prompt(q, kv, zb, table, batch, seq)
        xp = _layer_tail(xp, p_prompt[i].reshape(batch * seq, -1), ya, yb.reshape(batch * seq, db),
                         gates, w)

        ug, gv, q, kv, zb, gates = _layer_common(xs, w)
        ya, gs = _sgu(gv, ug, *sgu_w, batch=dec_batch, rows=dec_seq, keep_last_only=False)
        n_cache = cache_attn_k.shape[2]
        yb = _attn_sample(q, kv, zb, cache_attn_k[i].reshape(dec_batch, n_cache, db),
                          cache_attn_v[i].reshape(dec_batch, n_cache, db), table, dec_batch, dec_seq)
        xs = _layer_tail(xs, p_sample[i].reshape(dec_batch * dec_seq, -1), ya, yb, gates, w)

        keep = kp.shape[1]
        for lst, val in zip(outs, (kp.reshape(batch, keep, N_HEADS, dh),
                                   vp.reshape(batch, keep, N_HEADS, dh),
                                   kv[:, :db].reshape(dec_batch, dec_seq, N_HEADS, dh),
                                   kv[:, db:].reshape(dec_batch, dec_seq, N_HEADS, dh),
                                   gp, gs)):
            lst.append(val)
    return (xp.reshape(batch, seq, d), xs.reshape(dec_batch, dec_seq, d),
            *[jnp.stack(lst) for lst in outs])
```

```python
import functools
import math

import jax
import jax.numpy as jnp
from jax import lax
from jax.experimental import pallas as pl
from jax.experimental.pallas import tpu as pltpu

CHUNK = 64
GMLP_CHUNK = 128
GMLP_GROUPS = 8
N_HEADS = 16
PAST_CHUNKS = 8
WINDOW = PAST_CHUNKS * CHUNK
REL_CLIP = 128
EPS = 1e-6
NEG_INF = -1e30

LANES = 128
Q_BLOCK = 256
KEY_WIN = WINDOW + Q_BLOCK
VMEM_LIMIT_BYTES = 56 * 1024 * 1024

F32 = jnp.float32
BF16 = jnp.bfloat16


def _params(n_axes):
    return pltpu.CompilerParams(dimension_semantics=("arbitrary",) * n_axes,
                                vmem_limit_bytes=VMEM_LIMIT_BYTES)


def _tile(n, pref):
    if n <= pref:
        return n
    t = (pref // LANES) * LANES
    while n % t:
        t -= LANES
    return t


def _rmsnorm_kernel(x_ref, g_ref, o_ref):
    x = x_ref[...]
    ms = jnp.mean(x * x, axis=-1, keepdims=True)
    o_ref[...] = (x * lax.rsqrt(ms + EPS) * g_ref[...]).astype(o_ref.dtype)


def _rmsnorm_bf16(x, g):
    m, d = x.shape
    tr = _tile(m, 256)
    return pl.pallas_call(
        _rmsnorm_kernel,
        grid=(m // tr,),
        in_specs=[pl.BlockSpec((tr, d), lambda i: (i, 0)),
                  pl.BlockSpec((1, d), lambda i: (0, 0))],
        out_specs=pl.BlockSpec((tr, d), lambda i: (i, 0)),
        out_shape=jax.ShapeDtypeStruct((m, d), BF16),
        compiler_params=_params(1),
        name="pre_rmsnorm",
    )(x, g.reshape(1, d))


def _post_norm_kernel(o_ref, x_ref, g_ref, x1_ref, x1b_ref):
    o = o_ref[...]
    ms = jnp.mean(o * o, axis=-1, keepdims=True)
    x1 = x_ref[...] + o * lax.rsqrt(ms + EPS) * g_ref[...]
    x1_ref[...] = x1
    x1b_ref[...] = x1.astype(BF16)


def _post_norm_residual(o, x, g):
    m, d = x.shape
    tr = _tile(m, 256)
    row = pl.BlockSpec((tr, d), lambda i: (i, 0))
    return pl.pallas_call(
        _post_norm_kernel,
        grid=(m // tr,),
        in_specs=[row, row, pl.BlockSpec((1, d), lambda i: (0, 0))],
        out_specs=[row, row],
        out_shape=[jax.ShapeDtypeStruct((m, d), F32), jax.ShapeDtypeStruct((m, d), BF16)],
        compiler_params=_params(1),
        name="post_rmsnorm_residual",
    )(o, x, g.reshape(1, d))


def _proj_kernel(*refs, n_w, epilogue):
    h = refs[0][...]
    accs = [jnp.dot(h, w[...], preferred_element_type=F32) for w in refs[1:1 + n_w]]
    o_ref = refs[1 + n_w]
    o_ref[...] = epilogue(*accs).astype(o_ref.dtype)


def _col_map(i, j, *, first):
    return (0, first + j)


def _project(h, w, col_offsets, n_cols, epilogue, out_dtype, name, tn_pref=1024):
    m, k = h.shape
    tm = _tile(m, 1024)
    tn = _tile(math.gcd(n_cols, *col_offsets), tn_pref)
    n_w = len(col_offsets)
    w_specs = [pl.BlockSpec((k, tn), functools.partial(_col_map, first=off // tn))
               for off in col_offsets]
    return pl.pallas_call(
        functools.partial(_proj_kernel, n_w=n_w, epilogue=epilogue),
        grid=(m // tm, n_cols // tn),
        in_specs=[pl.BlockSpec((tm, k), lambda i, j: (i, 0))] + w_specs,
        out_specs=pl.BlockSpec((tm, tn), lambda i, j: (i, j)),
        out_shape=jax.ShapeDtypeStruct((m, n_cols), out_dtype),
        compiler_params=_params(2),
        name=name,
    )(h, *([w] * n_w))


def _gelu(x):
    return 0.5 * x * (1.0 + lax.erf(x * math.sqrt(0.5)))


def _gmlp_gate(u, z):
    return _gelu(u) * jax.nn.silu(z)


def _sgu_kernel(gv_ref, ug_ref, w_ref, bs_ref, lng_ref, lnb_ref, ya_ref, vn_ref, *, rows, gdim):
    x = gv_ref[...]
    xc = x - jnp.mean(x, axis=-1, keepdims=True)
    var = jnp.mean(xc * xc, axis=-1, keepdims=True)
    vn = xc * lax.rsqrt(var + EPS) * lng_ref[...] + lnb_ref[...]
    vn_ref[0] = vn
    kdim = w_ref.shape[2]
    vnb = vn.astype(BF16)
    if kdim > rows:
        vnb = jnp.concatenate([vnb, jnp.zeros((kdim - rows, vnb.shape[1]), BF16)], axis=0)
    i = lax.broadcasted_iota(jnp.int32, (rows, kdim), 0)
    j = lax.broadcasted_iota(jnp.int32, (rows, kdim), 1)
    visible = (j // CHUNK) <= (i // CHUNK)
    for g in range(w_ref.shape[0]):
        wg = jnp.where(visible, w_ref[g], 0.0).astype(BF16)
        sl = slice(g * gdim, (g + 1) * gdim)
        s = jnp.dot(wg, vnb[:, sl], preferred_element_type=F32) + bs_ref[:, g:g + 1]
        ya_ref[:, sl] = (ug_ref[:, sl].astype(F32) * s).astype(BF16)


def _sgu(gv, ug, w_s, b_s, ln_g, ln_b, batch, rows, keep_last_only):
    m, db = gv.shape
    groups = w_s.shape[0]
    n_chunks = m // (batch * rows)
    kdim = max(rows, LANES)
    w = w_s[:, :rows, :rows]
    if kdim > rows:
        w = jnp.pad(w, ((0, 0), (0, 0), (0, kdim - rows)))
    bs_t = b_s[:, :rows].T
    row_map = lambda b, c: (b * n_chunks + c, 0)
    full2 = lambda b, c: (0, 0)
    if keep_last_only:
        vn_shape, vn_spec = (batch, rows, db), pl.BlockSpec((1, rows, db), lambda b, c: (b, 0, 0))
    else:
        vn_shape = (batch * n_chunks, rows, db)
        vn_spec = pl.BlockSpec((1, rows, db), lambda b, c: (b * n_chunks + c, 0, 0))
    return pl.pallas_call(
        functools.partial(_sgu_kernel, rows=rows, gdim=db // groups),
        grid=(batch, n_chunks),
        in_specs=[pl.BlockSpec((rows, db), row_map),
                  pl.BlockSpec((rows, db), row_map),
                  pl.BlockSpec((groups, rows, kdim), lambda b, c: (0, 0, 0)),
                  pl.BlockSpec((rows, groups), full2),
                  pl.BlockSpec((1, db), full2),
                  pl.BlockSpec((1, db), full2)],
        out_specs=[pl.BlockSpec((rows, db), row_map), vn_spec],
        out_shape=[jax.ShapeDtypeStruct((m, db), BF16), jax.ShapeDtypeStruct(vn_shape, F32)],
        compiler_params=_params(2),
        name="sgu",
    )(gv, ug, w, bs_t, ln_g.reshape(1, db), ln_b.reshape(1, db))


def _bias_kernel(rb_ref, t_ref):
    wide = KEY_WIN + Q_BLOCK
    n_tab = rb_ref.shape[1]
    pos = lax.broadcasted_iota(jnp.int32, (n_tab, wide), 1)
    tab = lax.broadcasted_iota(jnp.int32, (n_tab, wide), 0)
    idx = jnp.clip(KEY_WIN - pos, -REL_CLIP, REL_CLIP) + REL_CLIP
    base = jnp.sum(jnp.where(tab == idx, rb_ref[0], 0.0), axis=0, keepdims=True)
    rolled = pltpu.roll(jnp.broadcast_to(base, (Q_BLOCK, wide)), 0, 1, stride=1, stride_axis=0)
    t = rolled[:, Q_BLOCK:]
    qc = lax.broadcasted_iota(jnp.int32, (Q_BLOCK, KEY_WIN), 0) // CHUNK
    kc = lax.broadcasted_iota(jnp.int32, (Q_BLOCK, KEY_WIN), 1) // CHUNK
    t_ref[0] = jnp.where((qc <= kc) & (kc <= qc + PAST_CHUNKS), t, NEG_INF)


def _bias_table(rel_bias):
    heads, n_rel = rel_bias.shape
    n_tab = -(-n_rel // 8) * 8
    rb = jnp.pad(rel_bias, ((0, 0), (0, n_tab - n_rel))).reshape(heads, n_tab, 1)
    return pl.pallas_call(
        _bias_kernel,
        grid=(heads,),
        in_specs=[pl.BlockSpec((1, n_tab, 1), lambda h: (h, 0, 0))],
        out_specs=pl.BlockSpec((1, Q_BLOCK, KEY_WIN), lambda h: (h, 0, 0)),
        out_shape=jax.ShapeDtypeStruct((heads, Q_BLOCK, KEY_WIN), F32),
        compiler_params=_params(1),
        name="attn_bias_table",
    )(rb)


def _softmax_pv(s, v):
    m = jnp.max(s, axis=-1, keepdims=True)
    p = jnp.exp(s - m)
    l = jnp.sum(p, axis=-1, keepdims=True)
    return jnp.dot(p.astype(BF16), v, preferred_element_type=F32) / l


_NT = (((1,), (1,)), ((), ()))


def _attn_prompt_kernel(q_ref, k_ref, v_ref, zb_ref, t_ref, y_ref, nk_ref, nv_ref, kb_s, vb_s,
                        *, seq, keep):
    kb_s[...] = k_ref[0].astype(BF16)
    vb_s[...] = v_ref[0].astype(BF16)
    nk_ref[0] = k_ref[0, seq - keep:, :]
    nv_ref[0] = v_ref[0, seq - keep:, :]

    for qs in range(0, seq, Q_BLOCK):
        ks = max(0, qs - WINDOW)
        rows, keys = slice(qs, qs + Q_BLOCK), slice(ks, qs + Q_BLOCK)
        bias = t_ref[0, :, KEY_WIN - (keys.stop - keys.start):]
        s = lax.dot_general(q_ref[0, rows, :], kb_s[keys, :], _NT,
                            preferred_element_type=F32) + bias
        o = _softmax_pv(s, vb_s[keys, :])
        y_ref[0, rows, :] = (o * zb_ref[0, rows, :].astype(F32)).astype(BF16)


def _attn_prompt(q, kv, zb, table, batch, seq):
    db = q.shape[1]
    dh = db // N_HEADS
    keep = min(WINDOW, seq)
    assert seq % Q_BLOCK == 0 and dh % LANES == 0
    shape3 = (batch, seq, db)
    kv3 = kv.reshape(batch, seq, 2 * db)
    head = pl.BlockSpec((1, seq, dh), lambda b, h: (b, 0, h))
    v_head = pl.BlockSpec((1, seq, dh), lambda b, h: (b, 0, N_HEADS + h))
    kept = pl.BlockSpec((1, keep, dh), lambda b, h: (b, 0, h))
    return pl.pallas_call(
        functools.partial(_attn_prompt_kernel, seq=seq, keep=keep),
        grid=(batch, N_HEADS),
        in_specs=[head, head, v_head, head,
                  pl.BlockSpec((1, Q_BLOCK, KEY_WIN), lambda b, h: (h, 0, 0))],
        out_specs=[head, kept, kept],
        out_shape=[jax.ShapeDtypeStruct(shape3, BF16),
                   jax.ShapeDtypeStruct((batch, keep, db), F32),
                   jax.ShapeDtypeStruct((batch, keep, db), F32)],
        scratch_shapes=[pltpu.VMEM((seq, dh), BF16), pltpu.VMEM((seq, dh), BF16)],
        compiler_params=_params(2),
        name="attn_prompt",
    )(q.reshape(shape3), kv3, kv3, zb.reshape(shape3), table)


def _attn_sample_kernel(q_ref, kn_ref, vn_ref, zb_ref, kc_ref, vc_ref, t_ref, y_ref, kcat_s, vcat_s,
                        *, n_new, n_cache, dh):
    n_keys = kcat_s.shape[0]
    n_real = n_cache + n_new
    pad = jnp.zeros((n_keys - n_real, dh), BF16)
    key_pos = lax.broadcasted_iota(jnp.int32, (n_new, n_keys), 1)
    for h in range(N_HEADS):
        sl = slice(h * dh, (h + 1) * dh)
        kcat_s[:n_cache, :] = kc_ref[0, :, sl].astype(BF16)
        kcat_s[n_cache:n_real, :] = kn_ref[0, :, sl].astype(BF16)
        kcat_s[n_real:, :] = pad
        vcat_s[:n_cache, :] = vc_ref[0, :, sl].astype(BF16)
        vcat_s[n_cache:n_real, :] = vn_ref[0, :, sl].astype(BF16)
        vcat_s[n_real:, :] = pad
        bias = jnp.where(key_pos < n_real, t_ref[h], NEG_INF)
        s = lax.dot_general(q_ref[0, :, sl], kcat_s[...], _NT, preferred_element_type=F32) + bias
        o = _softmax_pv(s, vcat_s[...])
        y_ref[0, :, sl] = (o * zb_ref[0, :, sl].astype(F32)).astype(BF16)


def _attn_sample(q, kv, zb, k_cache, v_cache, layer, table, batch, n_new):
    db = q.shape[1]
    dh = db // N_HEADS
    n_cache = k_cache.shape[1]
    assert n_cache == WINDOW and n_new <= CHUNK
    n_keys = -(-(n_cache + n_new) // LANES) * LANES
    new = pl.BlockSpec((1, n_new, db), lambda b: (b, 0, 0))
    new_v = pl.BlockSpec((1, n_new, db), lambda b: (b, 0, 1))
    cache = pl.BlockSpec((1, n_cache, db), lambda b: (layer * batch + b, 0, 0))
    shape3 = (batch, n_new, db)
    kv3 = kv.reshape(batch, n_new, 2 * db)
    y = pl.pallas_call(
        functools.partial(_attn_sample_kernel, n_new=n_new, n_cache=n_cache, dh=dh),
        grid=(batch,),
        in_specs=[new, new, new_v, new, cache, cache,
                  pl.BlockSpec((N_HEADS, n_new, n_keys), lambda b: (0, 0, 0))],
        out_specs=new,
        out_shape=jax.ShapeDtypeStruct(shape3, BF16),
        scratch_shapes=[pltpu.VMEM((n_keys, dh), BF16), pltpu.VMEM((n_keys, dh), BF16)],
        compiler_params=_params(1),
        name="attn_sample",
    )(q.reshape(shape3), kv3, kv3, zb.reshape(shape3), k_cache, v_cache, table[:, :n_new, :n_keys])
    return y.reshape(batch * n_new, db)


def _merge_kernel(ya_ref, yb_ref, wa_ref, wb_ref, ga_ref, gb_ref, m_ref):
    a = jnp.dot(ya_ref[...], wa_ref[...], preferred_element_type=F32)
    b = jnp.dot(yb_ref[...], wb_ref[...], preferred_element_type=F32)
    m_ref[...] = (ga_ref[...].astype(F32) * a + gb_ref[...].astype(F32) * b).astype(BF16)


def _merge(ya, yb, wa, wb, gates):
    m, kb = ya.shape
    n = wa.shape[1]
    tm, tn = _tile(m, 1024), _tile(n, 1024)
    lhs = pl.BlockSpec((tm, kb), lambda i, j: (i, 0))
    rhs = pl.BlockSpec((kb, tn), lambda i, j: (0, j))
    return pl.pallas_call(
        _merge_kernel,
        grid=(m // tm, n // tn),
        in_specs=[lhs, lhs, rhs, rhs,
                  pl.BlockSpec((tm, tn), lambda i, j: (i, j)),
                  pl.BlockSpec((tm, tn), functools.partial(lambda i, j, first: (i, first + j),
                                                           first=n // tn))],
        out_specs=pl.BlockSpec((tm, tn), lambda i, j: (i, j)),
        out_shape=jax.ShapeDtypeStruct((m, n), BF16),
        compiler_params=_params(2),
        name="merge",
    )(ya, yb, wa, wb, gates, gates)


def _ple_kernel(x1b_ref, wg_ref, p_ref, wp_ref, x1_ref, o_ref):
    gate = jax.nn.sigmoid(jnp.dot(x1b_ref[...], wg_ref[...], preferred_element_type=F32))
    proj = jnp.dot(p_ref[...].astype(BF16), wp_ref[...], preferred_element_type=F32)
    o_ref[...] = x1_ref[...] + gate * proj


def _ple(x1, x1b, p, w_pg, w_pp):
    m, d = x1.shape
    pd = p.shape[1]
    tm, tn = _tile(m, 1024), _tile(d, 512)
    tile = pl.BlockSpec((tm, tn), lambda i, j: (i, j))
    return pl.pallas_call(
        _ple_kernel,
        grid=(m // tm, d // tn),
        in_specs=[pl.BlockSpec((tm, d), lambda i, j: (i, 0)),
                  pl.BlockSpec((d, tn), lambda i, j: (0, j)),
                  pl.BlockSpec((tm, pd), lambda i, j: (i, 0)),
                  pl.BlockSpec((pd, tn), lambda i, j: (0, j)),
                  tile],
        out_specs=tile,
        out_shape=jax.ShapeDtypeStruct((m, d), F32),
        compiler_params=_params(2),
        name="ple",
    )(x1b, w_pg, p, w_pp, x1)


def _layer_common(x, w):
    d = x.shape[1]
    db = d // 2
    h = _rmsnorm_bf16(x, w["pre_g"])
    w_in = w["w_in"]
    ident = lambda a: a
    scale = (db // N_HEADS) ** -0.5
    ug = _project(h, w_in, (0, 2 * db), db, _gmlp_gate, BF16, "proj_gmlp_gate", tn_pref=512)
    gv = _project(h, w_in, (db,), db, _gelu, F32, "proj_gmlp_v")
    q = _project(h, w_in, (3 * db,), db, lambda a: a * scale, BF16, "proj_q")
    kv = _project(h, w_in, (4 * db,), 2 * db, ident, F32, "proj_kv")
    zb = _project(h, w_in, (6 * db,), db, jax.nn.silu, BF16, "proj_zb")
    gates = _project(h, w_in, (7 * db,), 2 * d, jax.nn.sigmoid, BF16, "proj_merge_gates")
    return ug, gv, q, kv, zb, gates


def _layer_tail(x, p, ya, yb, gates, w):
    m = _merge(ya, yb, w["w_up_a"], w["w_up_b"], gates)
    o = _project(m, w["w_out"], (0,), x.shape[1], lambda a: a, F32, "proj_out")
    x1, x1b = _post_norm_residual(o, x, w["post_g"])
    return _ple(x1, x1b, p, w["w_pg"], w["w_pp"])


def kernel(x_prompt, x_sample, cache_attn_k, cache_attn_v, p_prompt, p_sample, norm_pre_g, norm_post_g, w_in, gmlp_ln_g, gmlp_ln_b, gmlp_w_s, gmlp_b_s, attn_rel_bias, w_up_a, w_up_b, w_out, w_ple_gate, w_ple_proj):
    batch, seq, d = x_prompt.shape
    dec_batch, dec_seq, _ = x_sample.shape
    db = d // 2
    dh = db // N_HEADS
    depth = w_in.shape[0]
    assert seq % GMLP_CHUNK == 0 and dec_seq <= CHUNK

    xp = x_prompt.reshape(batch * seq, d)
    xs = x_sample.reshape(dec_batch * dec_seq, d)
    outs = [[] for _ in range(6)]
    for i in range(depth):
        w = dict(pre_g=norm_pre_g[i], post_g=norm_post_g[i], w_in=w_in[i].astype(BF16),
                 w_up_a=w_up_a[i].astype(BF16), w_up_b=w_up_b[i].astype(BF16),
                 w_out=w_out[i].astype(BF16), w_pg=w_ple_gate[i].astype(BF16),
                 w_pp=w_ple_proj[i].astype(BF16))
        table = _bias_table(attn_rel_bias[i])
        sgu_w = (gmlp_w_s[i], gmlp_b_s[i], gmlp_ln_g[i], gmlp_ln_b[i])

        ug, gv, q, kv, zb, gates = _layer_common(xp, w)
        ya, gp = _sgu(gv, ug, *sgu_w, batch=batch, rows=GMLP_CHUNK, keep_last_only=True)
        yb, kp, vp = _attn_prompt(q, kv, zb, table, batch, seq)
        xp = _layer_tail(xp, p_prompt[i].reshape(batch * seq, -1), ya, yb.reshape(batch * seq, db),
                         gates, w)

        ug, gv, q, kv, zb, gates = _layer_common(xs, w)
        ya, gs = _sgu(gv, ug, *sgu_w, batch=dec_batch, rows=dec_seq, keep_last_only=False)
        n_cache = cache_attn_k.shape[2]
        yb = _attn_sample(q, kv, zb, cache_attn_k.reshape(depth * dec_batch, n_cache, db),
                          cache_attn_v.reshape(depth * dec_batch, n_cache, db), i, table,
                          dec_batch, dec_seq)
        xs = _layer_tail(xs, p_sample[i].reshape(dec_batch * dec_seq, -1), ya, yb, gates, w)

        keep = kp.shape[1]
        for lst, val in zip(outs, (kp.reshape(batch, keep, N_HEADS, dh),
                                   vp.reshape(batch, keep, N_HEADS, dh),
                                   kv[:, :db].reshape(dec_batch, dec_seq, N_HEADS, dh),
                                   kv[:, db:].reshape(dec_batch, dec_seq, N_HEADS, dh),
                                   gp, gs)):
            lst.append(val)
    return (xp.reshape(batch, seq, d), xs.reshape(dec_batch, dec_seq, d),
            *[jnp.stack(lst) for lst in outs])
```

```python
import functools
import math

import jax
import jax.numpy as jnp
from jax import lax
from jax.experimental import pallas as pl
from jax.experimental.pallas import tpu as pltpu

CHUNK = 64
GMLP_CHUNK = 128
GMLP_GROUPS = 8
N_HEADS = 16
PAST_CHUNKS = 8
WINDOW = PAST_CHUNKS * CHUNK
REL_CLIP = 128
EPS = 1e-6
NEG_INF = -1e30

LANES = 128
Q_BLOCK = 256
KEY_WIN = WINDOW + Q_BLOCK
VMEM_LIMIT_BYTES = 56 * 1024 * 1024

F32 = jnp.float32
BF16 = jnp.bfloat16


def _params(n_axes):
    return pltpu.CompilerParams(dimension_semantics=("arbitrary",) * n_axes,
                                vmem_limit_bytes=VMEM_LIMIT_BYTES)


def _tile(n, pref):
    if n <= pref:
        return n
    t = (pref // LANES) * LANES
    while n % t:
        t -= LANES
    return t


def _rmsnorm_kernel(x_ref, g_ref, o_ref):
    x = x_ref[...]
    ms = jnp.mean(x * x, axis=-1, keepdims=True)
    o_ref[...] = (x * lax.rsqrt(ms + EPS) * g_ref[...]).astype(o_ref.dtype)


def _rmsnorm_bf16(x, g):
    m, d = x.shape
    tr = _tile(m, 256)
    return pl.pallas_call(
        _rmsnorm_kernel,
        grid=(m // tr,),
        in_specs=[pl.BlockSpec((tr, d), lambda i: (i, 0)),
                  pl.BlockSpec((1, d), lambda i: (0, 0))],
        out_specs=pl.BlockSpec((tr, d), lambda i: (i, 0)),
        out_shape=jax.ShapeDtypeStruct((m, d), BF16),
        compiler_params=_params(1),
        name="pre_rmsnorm",
    )(x, g.reshape(1, d))


def _proj_kernel(*refs, n_w, epilogue):
    h = refs[0][...]
    accs = [jnp.dot(h, w[...], preferred_element_type=F32) for w in refs[1:1 + n_w]]
    o_ref = refs[1 + n_w]
    o_ref[...] = epilogue(*accs).astype(o_ref.dtype)


def _col_map(i, j, *, first):
    return (0, first + j)


def _project(h, w, col_offsets, n_cols, epilogue, out_dtype, name, tn_pref=1024):
    m, k = h.shape
    tm = _tile(m, 1024)
    tn = _tile(math.gcd(n_cols, *col_offsets), tn_pref)
    n_w = len(col_offsets)
    w_specs = [pl.BlockSpec((k, tn), functools.partial(_col_map, first=off // tn))
               for off in col_offsets]
    return pl.pallas_call(
        functools.partial(_proj_kernel, n_w=n_w, epilogue=epilogue),
        grid=(m // tm, n_cols // tn),
        in_specs=[pl.BlockSpec((tm, k), lambda i, j: (i, 0))] + w_specs,
        out_specs=pl.BlockSpec((tm, tn), lambda i, j: (i, j)),
        out_shape=jax.ShapeDtypeStruct((m, n_cols), out_dtype),
        compiler_params=_params(2),
        name=name,
    )(h, *([w] * n_w))


def _gelu(x):
    return 0.5 * x * (1.0 + lax.erf(x * math.sqrt(0.5)))


def _gmlp_gate(u, z):
    return _gelu(u) * jax.nn.silu(z)


def _sgu_kernel(gv_ref, ug_ref, w_ref, bs_ref, lng_ref, lnb_ref, ya_ref, vn_ref, *, rows, gdim):
    x = gv_ref[...]
    xc = x - jnp.mean(x, axis=-1, keepdims=True)
    var = jnp.mean(xc * xc, axis=-1, keepdims=True)
    vn = xc * lax.rsqrt(var + EPS) * lng_ref[...] + lnb_ref[...]
    vn_ref[0] = vn
    kdim = w_ref.shape[2]
    vnb = vn.astype(BF16)
    if kdim > rows:
        vnb = jnp.concatenate([vnb, jnp.zeros((kdim - rows, vnb.shape[1]), BF16)], axis=0)
    i = lax.broadcasted_iota(jnp.int32, (rows, kdim), 0)
    j = lax.broadcasted_iota(jnp.int32, (rows, kdim), 1)
    visible = (j // CHUNK) <= (i // CHUNK)
    for g in range(w_ref.shape[0]):
        wg = jnp.where(visible, w_ref[g], 0.0).astype(BF16)
        sl = slice(g * gdim, (g + 1) * gdim)
        s = jnp.dot(wg, vnb[:, sl], preferred_element_type=F32) + bs_ref[:, g:g + 1]
        ya_ref[:, sl] = (ug_ref[:, sl].astype(F32) * s).astype(BF16)


def _sgu(gv, ug, w_s, b_s, ln_g, ln_b, batch, rows, keep_last_only):
    m, db = gv.shape
    groups = w_s.shape[0]
    n_chunks = m // (batch * rows)
    kdim = max(rows, LANES)
    w = w_s[:, :rows, :rows]
    if kdim > rows:
        w = jnp.pad(w, ((0, 0), (0, 0), (0, kdim - rows)))
    bs_t = b_s[:, :rows].T
    row_map = lambda b, c: (b * n_chunks + c, 0)
    full2 = lambda b, c: (0, 0)
    if keep_last_only:
        vn_shape, vn_spec = (batch, rows, db), pl.BlockSpec((1, rows, db), lambda b, c: (b, 0, 0))
    else:
        vn_shape = (batch * n_chunks, rows, db)
        vn_spec = pl.BlockSpec((1, rows, db), lambda b, c: (b * n_chunks + c, 0, 0))
    return pl.pallas_call(
        functools.partial(_sgu_kernel, rows=rows, gdim=db // groups),
        grid=(batch, n_chunks),
        in_specs=[pl.BlockSpec((rows, db), row_map),
                  pl.BlockSpec((rows, db), row_map),
                  pl.BlockSpec((groups, rows, kdim), lambda b, c: (0, 0, 0)),
                  pl.BlockSpec((rows, groups), full2),
                  pl.BlockSpec((1, db), full2),
                  pl.BlockSpec((1, db), full2)],
        out_specs=[pl.BlockSpec((rows, db), row_map), vn_spec],
        out_shape=[jax.ShapeDtypeStruct((m, db), BF16), jax.ShapeDtypeStruct(vn_shape, F32)],
        compiler_params=_params(2),
        name="sgu",
    )(gv, ug, w, bs_t, ln_g.reshape(1, db), ln_b.reshape(1, db))


def _bias_kernel(rb_ref, t_ref):
    wide = KEY_WIN + Q_BLOCK
    n_tab = rb_ref.shape[1]
    pos = lax.broadcasted_iota(jnp.int32, (n_tab, wide), 1)
    tab = lax.broadcasted_iota(jnp.int32, (n_tab, wide), 0)
    idx = jnp.clip(KEY_WIN - pos, -REL_CLIP, REL_CLIP) + REL_CLIP
    base = jnp.sum(jnp.where(tab == idx, rb_ref[0], 0.0), axis=0, keepdims=True)
    rolled = pltpu.roll(jnp.broadcast_to(base, (Q_BLOCK, wide)), 0, 1, stride=1, stride_axis=0)
    t = rolled[:, Q_BLOCK:]
    qc = lax.broadcasted_iota(jnp.int32, (Q_BLOCK, KEY_WIN), 0) // CHUNK
    kc = lax.broadcasted_iota(jnp.int32, (Q_BLOCK, KEY_WIN), 1) // CHUNK
    t_ref[0] = jnp.where((qc <= kc) & (kc <= qc + PAST_CHUNKS), t, NEG_INF)


def _bias_table(rel_bias):
    heads, n_rel = rel_bias.shape
    n_tab = -(-n_rel // 8) * 8
    rb = jnp.pad(rel_bias, ((0, 0), (0, n_tab - n_rel))).reshape(heads, n_tab, 1)
    return pl.pallas_call(
        _bias_kernel,
        grid=(heads,),
        in_specs=[pl.BlockSpec((1, n_tab, 1), lambda h: (h, 0, 0))],
        out_specs=pl.BlockSpec((1, Q_BLOCK, KEY_WIN), lambda h: (h, 0, 0)),
        out_shape=jax.ShapeDtypeStruct((heads, Q_BLOCK, KEY_WIN), F32),
        compiler_params=_params(1),
        name="attn_bias_table",
    )(rb)


def _softmax_pv(s, v):
    m = jnp.max(s, axis=-1, keepdims=True)
    p = jnp.exp(s - m)
    l = jnp.sum(p, axis=-1, keepdims=True)
    return jnp.dot(p.astype(BF16), v, preferred_element_type=F32) / l


_NT = (((1,), (1,)), ((), ()))


def _attn_prompt_kernel(q_ref, k_ref, v_ref, zb_ref, t_ref, y_ref, nk_ref, nv_ref, kb_s, vb_s,
                        *, seq, keep):
    kb_s[...] = k_ref[0].astype(BF16)
    vb_s[...] = v_ref[0].astype(BF16)
    nk_ref[0] = k_ref[0, seq - keep:, :]
    nv_ref[0] = v_ref[0, seq - keep:, :]

    for qs in range(0, seq, Q_BLOCK):
        ks = max(0, qs - WINDOW)
        rows, keys = slice(qs, qs + Q_BLOCK), slice(ks, qs + Q_BLOCK)
        bias = t_ref[0, :, KEY_WIN - (keys.stop - keys.start):]
        s = lax.dot_general(q_ref[0, rows, :], kb_s[keys, :], _NT,
                            preferred_element_type=F32) + bias
        o = _softmax_pv(s, vb_s[keys, :])
        y_ref[0, rows, :] = (o * zb_ref[0, rows, :].astype(F32)).astype(BF16)


def _attn_prompt(q, kv, zb, table, batch, seq):
    db = q.shape[1]
    dh = db // N_HEADS
    keep = min(WINDOW, seq)
    assert seq % Q_BLOCK == 0 and dh % LANES == 0
    shape3 = (batch, seq, db)
    kv3 = kv.reshape(batch, seq, 2 * db)
    head = pl.BlockSpec((1, seq, dh), lambda b, h: (b, 0, h))
    v_head = pl.BlockSpec((1, seq, dh), lambda b, h: (b, 0, N_HEADS + h))
    kept = pl.BlockSpec((1, keep, dh), lambda b, h: (b, 0, h))
    return pl.pallas_call(
        functools.partial(_attn_prompt_kernel, seq=seq, keep=keep),
        grid=(batch, N_HEADS),
        in_specs=[head, head, v_head, head,
                  pl.BlockSpec((1, Q_BLOCK, KEY_WIN), lambda b, h: (h, 0, 0))],
        out_specs=[head, kept, kept],
        out_shape=[jax.ShapeDtypeStruct(shape3, BF16),
                   jax.ShapeDtypeStruct((batch, keep, db), F32),
                   jax.ShapeDtypeStruct((batch, keep, db), F32)],
        scratch_shapes=[pltpu.VMEM((seq, dh), BF16), pltpu.VMEM((seq, dh), BF16)],
        compiler_params=_params(2),
        name="attn_prompt",
    )(q.reshape(shape3), kv3, kv3, zb.reshape(shape3), table)


HEAD_GROUP = 8


def _attn_sample_kernel(q_ref, kn_ref, vn_ref, zb_ref, kc_ref, vc_ref, b1_ref, b2_ref, y_ref):
    dh = q_ref.shape[-1]
    flat = lambda v: v.reshape(-1, dh)
    q = flat(q_ref[0]).astype(BF16)
    s_c = lax.dot_general(q, flat(kc_ref[0, 0]).astype(BF16), _NT,
                          preferred_element_type=F32) + b1_ref[0]
    s_n = lax.dot_general(q, flat(kn_ref[0]).astype(BF16), _NT,
                          preferred_element_type=F32) + b2_ref[0]
    m = jnp.maximum(jnp.max(s_c, axis=-1, keepdims=True), jnp.max(s_n, axis=-1, keepdims=True))
    p_c = jnp.exp(s_c - m)
    p_n = jnp.exp(s_n - m)
    l = jnp.sum(p_c, axis=-1, keepdims=True) + jnp.sum(p_n, axis=-1, keepdims=True)
    o = (jnp.dot(p_c.astype(BF16), flat(vc_ref[0, 0]).astype(BF16), preferred_element_type=F32)
         + jnp.dot(p_n.astype(BF16), flat(vn_ref[0]).astype(BF16), preferred_element_type=F32))
    y_ref[0] = (o / l * flat(zb_ref[0])).reshape(y_ref.shape[1:])


def _attn_sample(q, kv, zb, k_cache, v_cache, layer, table, batch, n_new):
    db = q.shape[1]
    dh = db // N_HEADS
    n_cache = k_cache.shape[2]
    groups = N_HEADS // HEAD_GROUP
    assert n_cache == WINDOW and n_new <= CHUNK and N_HEADS % HEAD_GROUP == 0
    n_pos = n_cache + n_new
    ts = table[:, :n_new, :n_pos].reshape(groups, HEAD_GROUP, n_new, n_pos).transpose(0, 2, 1, 3)
    same_head = jnp.eye(HEAD_GROUP, dtype=bool)[None, None, :, None, :]
    bias = jnp.where(same_head, ts[..., None], NEG_INF)
    rows_new, rows_cache = n_new * HEAD_GROUP, n_cache * HEAD_GROUP
    b_cache = bias[:, :, :, :n_cache].reshape(groups, rows_new, rows_cache)
    b_new = bias[:, :, :, n_cache:].reshape(groups, rows_new, rows_new)

    shape4 = (batch, n_new, N_HEADS, dh)
    new = pl.BlockSpec((1, n_new, HEAD_GROUP, dh), lambda g, b: (b, 0, g, 0))
    new_v = pl.BlockSpec((1, n_new, HEAD_GROUP, dh), lambda g, b: (b, 0, groups + g, 0))
    cache = pl.BlockSpec((1, 1, n_cache, HEAD_GROUP, dh), lambda g, b: (layer, b, 0, g, 0))
    kv4 = kv.reshape(batch, n_new, 2 * N_HEADS, dh)
    y = pl.pallas_call(
        _attn_sample_kernel,
        grid=(groups, batch),
        in_specs=[new, new, new_v, new, cache, cache,
                  pl.BlockSpec((1, rows_new, rows_cache), lambda g, b: (g, 0, 0)),
                  pl.BlockSpec((1, rows_new, rows_new), lambda g, b: (g, 0, 0))],
        out_specs=new,
        out_shape=jax.ShapeDtypeStruct(shape4, F32),
        compiler_params=_params(2),
        name="attn_sample",
    )(q.astype(F32).reshape(shape4), kv4, kv4, zb.astype(F32).reshape(shape4), k_cache, v_cache,
      b_cache, b_new)
    return y.reshape(batch * n_new, db).astype(BF16)


def _merge_kernel(ya_ref, yb_ref, wa_ref, wb_ref, ga_ref, gb_ref, m_ref):
    a = jnp.dot(ya_ref[...], wa_ref[...], preferred_element_type=F32)
    b = jnp.dot(yb_ref[...], wb_ref[...], preferred_element_type=F32)
    m_ref[...] = (ga_ref[...].astype(F32) * a + gb_ref[...].astype(F32) * b).astype(BF16)


def _merge(ya, yb, wa, wb, gates):
    m, kb = ya.shape
    n = wa.shape[1]
    tm, tn = _tile(m, 1024), _tile(n, 1024)
    lhs = pl.BlockSpec((tm, kb), lambda i, j: (i, 0))
    rhs = pl.BlockSpec((kb, tn), lambda i, j: (0, j))
    return pl.pallas_call(
        _merge_kernel,
        grid=(m // tm, n // tn),
        in_specs=[lhs, lhs, rhs, rhs,
                  pl.BlockSpec((tm, tn), lambda i, j: (i, j)),
                  pl.BlockSpec((tm, tn), functools.partial(lambda i, j, first: (i, first + j),
                                                           first=n // tn))],
        out_specs=pl.BlockSpec((tm, tn), lambda i, j: (i, j)),
        out_shape=jax.ShapeDtypeStruct((m, n), BF16),
        compiler_params=_params(2),
        name="merge",
    )(ya, yb, wa, wb, gates, gates)


def _tail_kernel(m_ref, w_ref, p_ref, wp_ref, g_ref, x_hbm, y_ref, o_s, x1b_s, ssq_s, xbuf, sem,
                 *, nj, tm, tn):
    i = pl.program_id(0)
    j = pl.program_id(1)
    x_copy = pltpu.make_async_copy(x_hbm.at[pl.ds(i * tm, tm), :], xbuf, sem)

    @pl.when(j == 0)
    def _():
        x_copy.start()
        ssq_s[...] = jnp.zeros_like(ssq_s)

    @pl.when(j < nj)
    def _():
        t = jnp.dot(m_ref[...], w_ref[...], preferred_element_type=F32)
        o_s[j] = t
        ssq_s[...] += jnp.sum(t * t, axis=-1, keepdims=True)

    @pl.when(j == nj)
    def _():
        x_copy.wait()
        rstd = lax.rsqrt(ssq_s[...] / (nj * tn) + EPS)
        for jb in range(nj):
            cols = slice(jb * tn, (jb + 1) * tn)
            x1 = xbuf[:, cols] + o_s[jb] * rstd * g_ref[:, cols]
            o_s[jb] = x1
            x1b_s[:, cols] = x1.astype(BF16)

    @pl.when(j >= nj)
    def _():
        gate = jax.nn.sigmoid(jnp.dot(x1b_s[...], w_ref[...], preferred_element_type=F32))
        proj = jnp.dot(p_ref[...].astype(BF16), wp_ref[...], preferred_element_type=F32)
        y_ref[...] = o_s[j - nj] + gate * proj


def _tail(m_act, x, p, w_cat, w_pp, g):
    m, d = x.shape
    pd = p.shape[1]
    tm, tn = _tile(m, 512), _tile(d, 1024)
    nj = d // tn
    second = lambda i, j: (0, jnp.maximum(j - nj, 0))
    return pl.pallas_call(
        functools.partial(_tail_kernel, nj=nj, tm=tm, tn=tn),
        grid=(m // tm, 2 * nj),
        in_specs=[pl.BlockSpec((tm, d), lambda i, j: (i, 0)),
                  pl.BlockSpec((d, tn), lambda i, j: (0, j)),
                  pl.BlockSpec((tm, pd), lambda i, j: (i, 0)),
                  pl.BlockSpec((pd, tn), second),
                  pl.BlockSpec((1, d), lambda i, j: (0, 0)),
                  pl.BlockSpec(memory_space=pl.ANY)],
        out_specs=pl.BlockSpec((tm, tn), lambda i, j: (i, jnp.maximum(j - nj, 0))),
        out_shape=jax.ShapeDtypeStruct((m, d), F32),
        scratch_shapes=[pltpu.VMEM((nj, tm, tn), F32),
                        pltpu.VMEM((tm, d), BF16),
                        pltpu.VMEM((tm, 1), F32),
                        pltpu.VMEM((tm, d), F32),
                        pltpu.SemaphoreType.DMA(())],
        compiler_params=_params(2),
        name="tail",
    )(m_act, w_cat, p, w_pp, g.reshape(1, d), x)


def _layer_common(x, w):
    d = x.shape[1]
    db = d // 2
    h = _rmsnorm_bf16(x, w["pre_g"])
    w_in = w["w_in"]
    ident = lambda a: a
    scale = (db // N_HEADS) ** -0.5
    ug = _project(h, w_in, (0, 2 * db), db, _gmlp_gate, BF16, "proj_gmlp_gate", tn_pref=512)
    gv = _project(h, w_in, (db,), db, _gelu, F32, "proj_gmlp_v")
    q = _project(h, w_in, (3 * db,), db, lambda a: a * scale, BF16, "proj_q")
    kv = _project(h, w_in, (4 * db,), 2 * db, ident, F32, "proj_kv")
    zb = _project(h, w_in, (6 * db,), db, jax.nn.silu, BF16, "proj_zb")
    gates = _project(h, w_in, (7 * db,), 2 * d, jax.nn.sigmoid, BF16, "proj_merge_gates")
    return ug, gv, q, kv, zb, gates


def _layer_tail(x, p, ya, yb, gates, w):
    m = _merge(ya, yb, w["w_up_a"], w["w_up_b"], gates)
    return _tail(m, x, p, w["w_out_pg"], w["w_pp"], w["post_g"])


def kernel(x_prompt, x_sample, cache_attn_k, cache_attn_v, p_prompt, p_sample, norm_pre_g, norm_post_g, w_in, gmlp_ln_g, gmlp_ln_b, gmlp_w_s, gmlp_b_s, attn_rel_bias, w_up_a, w_up_b, w_out, w_ple_gate, w_ple_proj):
    batch, seq, d = x_prompt.shape
    dec_batch, dec_seq, _ = x_sample.shape
    db = d // 2
    dh = db // N_HEADS
    depth = w_in.shape[0]
    assert seq % GMLP_CHUNK == 0 and dec_seq <= CHUNK

    xp = x_prompt.reshape(batch * seq, d)
    xs = x_sample.reshape(dec_batch * dec_seq, d)
    outs = [[] for _ in range(6)]
    for i in range(depth):
        w = dict(pre_g=norm_pre_g[i], post_g=norm_post_g[i], w_in=w_in[i].astype(BF16),
                 w_up_a=w_up_a[i].astype(BF16), w_up_b=w_up_b[i].astype(BF16),
                 w_out_pg=jnp.concatenate([w_out[i], w_ple_gate[i]], axis=1).astype(BF16),
                 w_pp=w_ple_proj[i].astype(BF16))
        table = _bias_table(attn_rel_bias[i])
        sgu_w = (gmlp_w_s[i], gmlp_b_s[i], gmlp_ln_g[i], gmlp_ln_b[i])

        ug, gv, q, kv, zb, gates = _layer_common(xp, w)
        ya, gp = _sgu(gv, ug, *sgu_w, batch=batch, rows=GMLP_CHUNK, keep_last_only=True)
        yb, kp, vp = _attn_prompt(q, kv, zb, table, batch, seq)
        xp = _layer_tail(xp, p_prompt[i].reshape(batch * seq, -1), ya, yb.reshape(batch * seq, db),
                         gates, w)

        ug, gv, q, kv, zb, gates = _layer_common(xs, w)
        ya, gs = _sgu(gv, ug, *sgu_w, batch=dec_batch, rows=dec_seq, keep_last_only=False)
        yb = _attn_sample(q, kv, zb, cache_attn_k, cache_attn_v, i, table, dec_batch, dec_seq)
        xs = _layer_tail(xs, p_sample[i].reshape(dec_batch * dec_seq, -1), ya, yb, gates, w)

        keep = kp.shape[1]
        for lst, val in zip(outs, (kp.reshape(batch, keep, N_HEADS, dh),
                                   vp.reshape(batch, keep, N_HEADS, dh),
                                   kv[:, :db].reshape(dec_batch, dec_seq, N_HEADS, dh),
                                   kv[:, db:].reshape(dec_batch, dec_seq, N_HEADS, dh),
                                   gp, gs)):
            lst.append(val)
    return (xp.reshape(batch, seq, d), xs.reshape(dec_batch, dec_seq, d),
            *[jnp.stack(lst) for lst in outs])
```

```python
import functools
import math

import jax
import jax.numpy as jnp
from jax import lax
from jax.experimental import pallas as pl
from jax.experimental.pallas import tpu as pltpu

CHUNK = 64
GMLP_CHUNK = 128
GMLP_GROUPS = 8
N_HEADS = 16
PAST_CHUNKS = 8
WINDOW = PAST_CHUNKS * CHUNK
REL_CLIP = 128
EPS = 1e-6
NEG_INF = -1e30

LANES = 128
Q_BLOCK = 256
KEY_WIN = WINDOW + Q_BLOCK
VMEM_LIMIT_BYTES = 56 * 1024 * 1024

F32 = jnp.float32
BF16 = jnp.bfloat16


def _params(n_axes):
    return pltpu.CompilerParams(dimension_semantics=("arbitrary",) * n_axes,
                                vmem_limit_bytes=VMEM_LIMIT_BYTES)


def _tile(n, pref):
    if n <= pref:
        return n
    t = (pref // LANES) * LANES
    while n % t:
        t -= LANES
    return t


def _rmsnorm_kernel(x_ref, g_ref, o_ref):
    x = x_ref[...]
    ms = jnp.mean(x * x, axis=-1, keepdims=True)
    o_ref[...] = (x * lax.rsqrt(ms + EPS) * g_ref[...]).astype(o_ref.dtype)


def _rmsnorm_bf16(x, g):
    m, d = x.shape
    tr = _tile(m, 256)
    return pl.pallas_call(
        _rmsnorm_kernel,
        grid=(m // tr,),
        in_specs=[pl.BlockSpec((tr, d), lambda i: (i, 0)),
                  pl.BlockSpec((1, d), lambda i: (0, 0))],
        out_specs=pl.BlockSpec((tr, d), lambda i: (i, 0)),
        out_shape=jax.ShapeDtypeStruct((m, d), BF16),
        compiler_params=_params(1),
        name="pre_rmsnorm",
    )(x, g.reshape(1, d))


def _proj_kernel(*refs, n_w, epilogue, side_layout):
    h = refs[0][...]
    accs = [jnp.dot(h, w[...], preferred_element_type=F32) for w in refs[1:1 + n_w]]
    n_side_in = sum(len(widths) for widths in side_layout)
    side_in = refs[1 + n_w:1 + n_w + n_side_in]
    o_ref = refs[1 + n_w + n_side_in]
    side_out = refs[2 + n_w + n_side_in:]
    o_ref[...] = epilogue(*accs).astype(o_ref.dtype)
    k = 0
    for out_ref, widths in zip(side_out, side_layout):
        off = 0
        for width in widths:
            out_ref[:, off:off + width] = side_in[k][...].astype(BF16)
            off += width
            k += 1


def _col_map(i, j, *, first):
    return (0, first + j)


def _slab_map(i, j, *, nj, col):
    return (i * nj + j, col)


def _project(h, w, col_offsets, n_cols, epilogue, out_dtype, name, tn_pref=1024, side=()):
    m, k = h.shape
    tm = _tile(m, 1024)
    tn = _tile(math.gcd(n_cols, *col_offsets), tn_pref)
    n_w = len(col_offsets)
    ni, nj = m // tm, n_cols // tn
    w_specs = [pl.BlockSpec((k, tn), functools.partial(_col_map, first=off // tn))
               for off in col_offsets]
    side_in, side_specs, side_out_specs, side_shapes, layout = [], [], [], [], []
    for group in side:
        rows = group[0][0].shape[0]
        slab = rows // (ni * nj)
        assert slab * ni * nj == rows and slab % 16 == 0
        total = sum(width for _, _, width in group)
        for src, col0, width in group:
            assert col0 % width == 0
            side_in.append(src)
            side_specs.append(pl.BlockSpec((slab, width),
                                           functools.partial(_slab_map, nj=nj, col=col0 // width)))
        side_out_specs.append(pl.BlockSpec((slab, total), functools.partial(_slab_map, nj=nj, col=0)))
        side_shapes.append(jax.ShapeDtypeStruct((rows, total), BF16))
        layout.append(tuple(width for _, _, width in group))
    outs = pl.pallas_call(
        functools.partial(_proj_kernel, n_w=n_w, epilogue=epilogue, side_layout=tuple(layout)),
        grid=(ni, nj),
        in_specs=[pl.BlockSpec((tm, k), lambda i, j: (i, 0))] + w_specs + side_specs,
        out_specs=[pl.BlockSpec((tm, tn), lambda i, j: (i, j))] + side_out_specs,
        out_shape=[jax.ShapeDtypeStruct((m, n_cols), out_dtype)] + side_shapes,
        compiler_params=_params(2),
        name=name,
    )(h, *([w] * n_w), *side_in)
    return outs[0] if not side else outs


def _gelu(x):
    return 0.5 * x * (1.0 + lax.erf(x * math.sqrt(0.5)))


def _gmlp_gate(u, z):
    return _gelu(u) * jax.nn.silu(z)


def _sgu_kernel(gv_ref, ug_ref, w_ref, bs_ref, lng_ref, lnb_ref, ya_ref, vn_ref, *, rows, gdim):
    x = gv_ref[...]
    xc = x - jnp.mean(x, axis=-1, keepdims=True)
    var = jnp.mean(xc * xc, axis=-1, keepdims=True)
    vn = xc * lax.rsqrt(var + EPS) * lng_ref[...] + lnb_ref[...]
    vn_ref[0] = vn
    kdim = w_ref.shape[2]
    vnb = vn.astype(BF16)
    if kdim > rows:
        vnb = jnp.concatenate([vnb, jnp.zeros((kdim - rows, vnb.shape[1]), BF16)], axis=0)
    i = lax.broadcasted_iota(jnp.int32, (rows, kdim), 0)
    j = lax.broadcasted_iota(jnp.int32, (rows, kdim), 1)
    visible = (j // CHUNK) <= (i // CHUNK)
    for g in range(w_ref.shape[0]):
        wg = jnp.where(visible, w_ref[g], 0.0).astype(BF16)
        sl = slice(g * gdim, (g + 1) * gdim)
        s = jnp.dot(wg, vnb[:, sl], preferred_element_type=F32) + bs_ref[:, g:g + 1]
        ya_ref[:, sl] = (ug_ref[:, sl].astype(F32) * s).astype(BF16)


def _sgu(gv, ug, w_s, b_s, ln_g, ln_b, batch, rows, keep_last_only):
    m, db = gv.shape
    groups = w_s.shape[0]
    n_chunks = m // (batch * rows)
    kdim = max(rows, LANES)
    w = w_s[:, :rows, :rows]
    if kdim > rows:
        w = jnp.pad(w, ((0, 0), (0, 0), (0, kdim - rows)))
    bs_t = b_s[:, :rows].T
    row_map = lambda b, c: (b * n_chunks + c, 0)
    full2 = lambda b, c: (0, 0)
    if keep_last_only:
        vn_shape, vn_spec = (batch, rows, db), pl.BlockSpec((1, rows, db), lambda b, c: (b, 0, 0))
    else:
        vn_shape = (batch * n_chunks, rows, db)
        vn_spec = pl.BlockSpec((1, rows, db), lambda b, c: (b * n_chunks + c, 0, 0))
    return pl.pallas_call(
        functools.partial(_sgu_kernel, rows=rows, gdim=db // groups),
        grid=(batch, n_chunks),
        in_specs=[pl.BlockSpec((rows, db), row_map),
                  pl.BlockSpec((rows, db), row_map),
                  pl.BlockSpec((groups, rows, kdim), lambda b, c: (0, 0, 0)),
                  pl.BlockSpec((rows, groups), full2),
                  pl.BlockSpec((1, db), full2),
                  pl.BlockSpec((1, db), full2)],
        out_specs=[pl.BlockSpec((rows, db), row_map), vn_spec],
        out_shape=[jax.ShapeDtypeStruct((m, db), BF16), jax.ShapeDtypeStruct(vn_shape, F32)],
        compiler_params=_params(2),
        name="sgu",
    )(gv, ug, w, bs_t, ln_g.reshape(1, db), ln_b.reshape(1, db))


def _bias_kernel(rb_ref, t_ref):
    wide = KEY_WIN + Q_BLOCK
    n_tab = rb_ref.shape[1]
    pos = lax.broadcasted_iota(jnp.int32, (n_tab, wide), 1)
    tab = lax.broadcasted_iota(jnp.int32, (n_tab, wide), 0)
    idx = jnp.clip(KEY_WIN - pos, -REL_CLIP, REL_CLIP) + REL_CLIP
    base = jnp.sum(jnp.where(tab == idx, rb_ref[0], 0.0), axis=0, keepdims=True)
    rolled = pltpu.roll(jnp.broadcast_to(base, (Q_BLOCK, wide)), 0, 1, stride=1, stride_axis=0)
    t = rolled[:, Q_BLOCK:]
    qc = lax.broadcasted_iota(jnp.int32, (Q_BLOCK, KEY_WIN), 0) // CHUNK
    kc = lax.broadcasted_iota(jnp.int32, (Q_BLOCK, KEY_WIN), 1) // CHUNK
    t_ref[0] = jnp.where((qc <= kc) & (kc <= qc + PAST_CHUNKS), t, NEG_INF)


def _bias_table(rel_bias):
    heads, n_rel = rel_bias.shape
    n_tab = -(-n_rel // 8) * 8
    rb = jnp.pad(rel_bias, ((0, 0), (0, n_tab - n_rel))).reshape(heads, n_tab, 1)
    return pl.pallas_call(
        _bias_kernel,
        grid=(heads,),
        in_specs=[pl.BlockSpec((1, n_tab, 1), lambda h: (h, 0, 0))],
        out_specs=pl.BlockSpec((1, Q_BLOCK, KEY_WIN), lambda h: (h, 0, 0)),
        out_shape=jax.ShapeDtypeStruct((heads, Q_BLOCK, KEY_WIN), F32),
        compiler_params=_params(1),
        name="attn_bias_table",
    )(rb)


def _softmax_pv(s, v):
    m = jnp.max(s, axis=-1, keepdims=True)
    p = jnp.exp(s - m)
    l = jnp.sum(p, axis=-1, keepdims=True)
    return jnp.dot(p.astype(BF16), v, preferred_element_type=F32) / l


_NT = (((1,), (1,)), ((), ()))


def _attn_prompt_kernel(q_ref, k_ref, v_ref, zb_ref, t_ref, y_ref, nk_ref, nv_ref, kb_s, vb_s,
                        *, seq, keep):
    kb_s[...] = k_ref[0].astype(BF16)
    vb_s[...] = v_ref[0].astype(BF16)
    nk_ref[0] = k_ref[0, seq - keep:, :]
    nv_ref[0] = v_ref[0, seq - keep:, :]

    for qs in range(0, seq, Q_BLOCK):
        ks = max(0, qs - WINDOW)
        rows, keys = slice(qs, qs + Q_BLOCK), slice(ks, qs + Q_BLOCK)
        bias = t_ref[0, :, KEY_WIN - (keys.stop - keys.start):]
        s = lax.dot_general(q_ref[0, rows, :], kb_s[keys, :], _NT,
                            preferred_element_type=F32) + bias
        o = _softmax_pv(s, vb_s[keys, :])
        y_ref[0, rows, :] = (o * zb_ref[0, rows, :].astype(F32)).astype(BF16)


def _attn_prompt(q, kv, zb, table, batch, seq):
    db = q.shape[1]
    dh = db // N_HEADS
    keep = min(WINDOW, seq)
    assert seq % Q_BLOCK == 0 and dh % LANES == 0
    shape3 = (batch, seq, db)
    kv3 = kv.reshape(batch, seq, 2 * db)
    head = pl.BlockSpec((1, seq, dh), lambda b, h: (b, 0, h))
    v_head = pl.BlockSpec((1, seq, dh), lambda b, h: (b, 0, N_HEADS + h))
    kept = pl.BlockSpec((1, keep, dh), lambda b, h: (b, 0, h))
    return pl.pallas_call(
        functools.partial(_attn_prompt_kernel, seq=seq, keep=keep),
        grid=(batch, N_HEADS),
        in_specs=[head, head, v_head, head,
                  pl.BlockSpec((1, Q_BLOCK, KEY_WIN), lambda b, h: (h, 0, 0))],
        out_specs=[head, kept, kept],
        out_shape=[jax.ShapeDtypeStruct(shape3, BF16),
                   jax.ShapeDtypeStruct((batch, keep, db), F32),
                   jax.ShapeDtypeStruct((batch, keep, db), F32)],
        scratch_shapes=[pltpu.VMEM((seq, dh), BF16), pltpu.VMEM((seq, dh), BF16)],
        compiler_params=_params(2),
        name="attn_prompt",
    )(q.reshape(shape3), kv3, kv3, zb.reshape(shape3), table)


HEAD_GROUP = 8


def _attn_sample_kernel(q_ref, kn_ref, vn_ref, zb_ref, kc_ref, vc_ref, b1_ref, b2_ref, y_ref):
    dh = q_ref.shape[-1]
    flat = lambda v: v.reshape(-1, dh)
    q = flat(q_ref[0]).astype(BF16)
    s_c = lax.dot_general(q, flat(kc_ref[0, 0]).astype(BF16), _NT,
                          preferred_element_type=F32) + b1_ref[0]
    s_n = lax.dot_general(q, flat(kn_ref[0]).astype(BF16), _NT,
                          preferred_element_type=F32) + b2_ref[0]
    m = jnp.maximum(jnp.max(s_c, axis=-1, keepdims=True), jnp.max(s_n, axis=-1, keepdims=True))
    p_c = jnp.exp(s_c - m)
    p_n = jnp.exp(s_n - m)
    l = jnp.sum(p_c, axis=-1, keepdims=True) + jnp.sum(p_n, axis=-1, keepdims=True)
    o = (jnp.dot(p_c.astype(BF16), flat(vc_ref[0, 0]).astype(BF16), preferred_element_type=F32)
         + jnp.dot(p_n.astype(BF16), flat(vn_ref[0]).astype(BF16), preferred_element_type=F32))
    y_ref[0] = (o / l * flat(zb_ref[0])).reshape(y_ref.shape[1:])


def _sample_bias_kernel(a_ref, b_ref):
    rows = a_ref.shape[1]
    pos = lax.broadcasted_iota(jnp.int32, (LANES, LANES * HEAD_GROUP), 0)
    col = lax.broadcasted_iota(jnp.int32, (LANES, LANES * HEAD_GROUP), 1)
    expand = (col // HEAD_GROUP == pos).astype(F32)
    r = lax.broadcasted_iota(jnp.int32, (rows, LANES * HEAD_GROUP), 0)
    c = lax.broadcasted_iota(jnp.int32, (rows, LANES * HEAD_GROUP), 1)
    same_head = (r % HEAD_GROUP) == (c % HEAD_GROUP)
    for blk in range(a_ref.shape[2] // LANES):
        wide = jnp.dot(a_ref[0, :, blk * LANES:(blk + 1) * LANES], expand,
                       preferred_element_type=F32, precision=lax.Precision.HIGHEST)
        b_ref[0, :, blk * LANES * HEAD_GROUP:(blk + 1) * LANES * HEAD_GROUP] = jnp.where(
            same_head, wide, NEG_INF)


def _sample_bias(table, n_new, n_pos):
    groups = N_HEADS // HEAD_GROUP
    rows = n_new * HEAD_GROUP
    n_pad = -(-n_pos // LANES) * LANES
    a = table[:, :n_new, :n_pos].reshape(groups, HEAD_GROUP, n_new, n_pos).transpose(0, 2, 1, 3)
    a = jnp.pad(a.reshape(groups, rows, n_pos), ((0, 0), (0, 0), (0, n_pad - n_pos)))
    return pl.pallas_call(
        _sample_bias_kernel,
        grid=(groups,),
        in_specs=[pl.BlockSpec((1, rows, n_pad), lambda g: (g, 0, 0))],
        out_specs=pl.BlockSpec((1, rows, n_pad * HEAD_GROUP), lambda g: (g, 0, 0)),
        out_shape=jax.ShapeDtypeStruct((groups, rows, n_pad * HEAD_GROUP), F32),
        compiler_params=_params(1),
        name="attn_sample_bias",
    )(a)


def _attn_sample(q, kv, zb, k_cache, v_cache, layer, table, batch, n_new):
    db = q.shape[1]
    dh = db // N_HEADS
    n_cache = k_cache.shape[2]
    groups = N_HEADS // HEAD_GROUP
    assert n_cache == WINDOW and n_new <= CHUNK and N_HEADS % HEAD_GROUP == 0
    rows_new, rows_cache = n_new * HEAD_GROUP, n_cache * HEAD_GROUP
    bias = _sample_bias(table, n_new, n_cache + n_new)

    shape4 = (batch, n_new, N_HEADS, dh)
    new = pl.BlockSpec((1, n_new, HEAD_GROUP, dh), lambda g, b: (b, 0, g, 0))
    new_v = pl.BlockSpec((1, n_new, HEAD_GROUP, dh), lambda g, b: (b, 0, groups + g, 0))
    cache = pl.BlockSpec((1, 1, n_cache, HEAD_GROUP, dh), lambda g, b: (layer, b, 0, g, 0))
    kv4 = kv.reshape(batch, n_new, 2 * N_HEADS, dh)
    y = pl.pallas_call(
        _attn_sample_kernel,
        grid=(groups, batch),
        in_specs=[new, new, new_v, new, cache, cache,
                  pl.BlockSpec((1, rows_new, rows_cache), lambda g, b: (g, 0, 0)),
                  pl.BlockSpec((1, rows_new, rows_new), lambda g, b: (g, 0, rows_cache // rows_new))],
        out_specs=new,
        out_shape=jax.ShapeDtypeStruct(shape4, F32),
        compiler_params=_params(2),
        name="attn_sample",
    )(q.astype(F32).reshape(shape4), kv4, kv4, zb.astype(F32).reshape(shape4), k_cache, v_cache,
      bias, bias)
    return y.reshape(batch * n_new, db).astype(BF16)


def _merge_kernel(ya_ref, yb_ref, wa_ref, wb_ref, ga_ref, gb_ref, m_ref):
    a = jnp.dot(ya_ref[...], wa_ref[...], preferred_element_type=F32)
    b = jnp.dot(yb_ref[...], wb_ref[...], preferred_element_type=F32)
    m_ref[...] = (ga_ref[...].astype(F32) * a + gb_ref[...].astype(F32) * b).astype(BF16)


def _merge(ya, yb, w_up, gates):
    m, kb = ya.shape
    n = w_up.shape[1] // 2
    tm, tn = _tile(m, 1024), _tile(n, 1024)
    nj = n // tn
    lhs = pl.BlockSpec((tm, kb), lambda i, j: (i, 0))
    rhs_a = pl.BlockSpec((kb, tn), lambda i, j: (0, j))
    rhs_b = pl.BlockSpec((kb, tn), lambda i, j: (0, nj + j))
    return pl.pallas_call(
        _merge_kernel,
        grid=(m // tm, nj),
        in_specs=[lhs, lhs, rhs_a, rhs_b,
                  pl.BlockSpec((tm, tn), lambda i, j: (i, j)),
                  pl.BlockSpec((tm, tn), lambda i, j: (i, nj + j))],
        out_specs=pl.BlockSpec((tm, tn), lambda i, j: (i, j)),
        out_shape=jax.ShapeDtypeStruct((m, n), BF16),
        compiler_params=_params(2),
        name="merge",
    )(ya, yb, w_up, w_up, gates, gates)


def _tail_kernel(m_ref, w_ref, p_ref, wp_ref, g_ref, x_hbm, y_ref, o_s, x1b_s, ssq_s, xbuf, sem,
                 *, nj, tm, tn):
    i = pl.program_id(0)
    j = pl.program_id(1)
    x_copy = pltpu.make_async_copy(x_hbm.at[pl.ds(i * tm, tm), :], xbuf, sem)

    @pl.when(j == 0)
    def _():
        x_copy.start()
        ssq_s[...] = jnp.zeros_like(ssq_s)

    @pl.when(j < nj)
    def _():
        t = jnp.dot(m_ref[...], w_ref[...], preferred_element_type=F32)
        o_s[j] = t
        ssq_s[...] += jnp.sum(t * t, axis=-1, keepdims=True)

    @pl.when(j == nj)
    def _():
        x_copy.wait()
        rstd = lax.rsqrt(ssq_s[...] / (nj * tn) + EPS)
        for jb in range(nj):
            cols = slice(jb * tn, (jb + 1) * tn)
            x1 = xbuf[:, cols] + o_s[jb] * rstd * g_ref[:, cols]
            o_s[jb] = x1
            x1b_s[:, cols] = x1.astype(BF16)

    @pl.when(j >= nj)
    def _():
        gate = jax.nn.sigmoid(jnp.dot(x1b_s[...], w_ref[...], preferred_element_type=F32))
        proj = jnp.dot(p_ref[...].astype(BF16), wp_ref[...], preferred_element_type=F32)
        y_ref[...] = o_s[j - nj] + gate * proj


def _tail(m_act, x, p, w_cat, w_pp, g):
    m, d = x.shape
    pd = p.shape[1]
    tm, tn = _tile(m, 512), _tile(d, 1024)
    nj = d // tn
    second = lambda i, j: (0, jnp.maximum(j - nj, 0))
    return pl.pallas_call(
        functools.partial(_tail_kernel, nj=nj, tm=tm, tn=tn),
        grid=(m // tm, 2 * nj),
        in_specs=[pl.BlockSpec((tm, d), lambda i, j: (i, 0)),
                  pl.BlockSpec((d, tn), lambda i, j: (0, j)),
                  pl.BlockSpec((tm, pd), lambda i, j: (i, 0)),
                  pl.BlockSpec((pd, tn), second),
                  pl.BlockSpec((1, d), lambda i, j: (0, 0)),
                  pl.BlockSpec(memory_space=pl.ANY)],
        out_specs=pl.BlockSpec((tm, tn), lambda i, j: (i, jnp.maximum(j - nj, 0))),
        out_shape=jax.ShapeDtypeStruct((m, d), F32),
        scratch_shapes=[pltpu.VMEM((nj, tm, tn), F32),
                        pltpu.VMEM((tm, d), BF16),
                        pltpu.VMEM((tm, 1), F32),
                        pltpu.VMEM((tm, d), F32),
                        pltpu.SemaphoreType.DMA(())],
        compiler_params=_params(2),
        name="tail",
    )(m_act, w_cat, p, w_pp, g.reshape(1, d), x)


def _layer_common(x, pre_g, wb, pending):
    d = x.shape[1]
    db = d // 2
    h = _rmsnorm_bf16(x, pre_g)
    scale = (db // N_HEADS) ** -0.5

    def run(wname, offs, n_cols, epilogue, out_dtype, name, then=(), tn_pref=1024):
        todo = [n for n in then if n in pending]
        outs = _project(h, wb[wname], offs, n_cols, epilogue, out_dtype, name, tn_pref,
                        side=[pending.pop(n) for n in todo])
        if not todo:
            return outs
        wb.update(zip(todo, outs[1:]))
        return outs[0]

    ug = run("gate", (0, db), db, _gmlp_gate, BF16, "proj_gmlp_gate", then=("v",), tn_pref=512)
    gv = run("v", (0,), db, _gelu, F32, "proj_gmlp_v", then=("q",))
    q = run("q", (0,), db, lambda a: a * scale, BF16, "proj_q", then=("kv", "zb"))
    kv = run("kv", (0,), 2 * db, lambda a: a, F32, "proj_kv", then=("gates",))
    zb = run("zb", (0,), db, jax.nn.silu, BF16, "proj_zb")
    gates = run("gates", (0,), 2 * d, jax.nn.sigmoid, BF16, "proj_merge_gates", then=("up", "tail"))
    return ug, gv, q, kv, zb, gates


def _layer_tail(x, p, ya, yb, gates, wb, post_g, w_pp):
    m = _merge(ya, yb, wb["up"], gates)
    return _tail(m, x, p, wb["tail"], w_pp, post_g)


def kernel(x_prompt, x_sample, cache_attn_k, cache_attn_v, p_prompt, p_sample, norm_pre_g, norm_post_g, w_in, gmlp_ln_g, gmlp_ln_b, gmlp_w_s, gmlp_b_s, attn_rel_bias, w_up_a, w_up_b, w_out, w_ple_gate, w_ple_proj):
    batch, seq, d = x_prompt.shape
    dec_batch, dec_seq, _ = x_sample.shape
    db = d // 2
    dh = db // N_HEADS
    depth = w_in.shape[0]
    assert seq % GMLP_CHUNK == 0 and dec_seq <= CHUNK

    xp = x_prompt.reshape(batch * seq, d)
    xs = x_sample.reshape(dec_batch * dec_seq, d)
    outs = [[] for _ in range(6)]
    for i in range(depth):
        wi = w_in[i]
        seg = lambda first, n=1: (wi, first * db, n * db)
        wb = {"gate": jnp.concatenate([wi[:, :db], wi[:, 2 * db:3 * db]], axis=1).astype(BF16)}
        pending = {"v": [seg(1)], "q": [seg(3)], "kv": [seg(4, 2)], "zb": [seg(6)],
                   "gates": [seg(7), seg(8), seg(9), seg(10)],
                   "up": [(w_up_a[i], 0, d), (w_up_b[i], 0, d)],
                   "tail": [(w_out[i], 0, d), (w_ple_gate[i], 0, d)]}
        w_pp = w_ple_proj[i].astype(BF16)
        table = _bias_table(attn_rel_bias[i])
        sgu_w = (gmlp_w_s[i], gmlp_b_s[i], gmlp_ln_g[i], gmlp_ln_b[i])

        ug, gv, q, kv, zb, gates = _layer_common(xp, norm_pre_g[i], wb, pending)
        ya, gp = _sgu(gv, ug, *sgu_w, batch=batch, rows=GMLP_CHUNK, keep_last_only=True)
        yb, kp, vp = _attn_prompt(q, kv, zb, table, batch, seq)
        xp = _layer_tail(xp, p_prompt[i].reshape(batch * seq, -1), ya, yb.reshape(batch * seq, db),
                         gates, wb, norm_post_g[i], w_pp)

        ug, gv, q, kv, zb, gates = _layer_common(xs, norm_pre_g[i], wb, pending)
        ya, gs = _sgu(gv, ug, *sgu_w, batch=dec_batch, rows=dec_seq, keep_last_only=False)
        yb = _attn_sample(q, kv, zb, cache_attn_k, cache_attn_v, i, table, dec_batch, dec_seq)
        xs = _layer_tail(xs, p_sample[i].reshape(dec_batch * dec_seq, -1), ya, yb, gates, wb,
                         norm_post_g[i], w_pp)

        keep = kp.shape[1]
        for lst, val in zip(outs, (kp.reshape(batch, keep, N_HEADS, dh),
                                   vp.reshape(batch, keep, N_HEADS, dh),
                                   kv[:, :db].reshape(dec_batch, dec_seq, N_HEADS, dh),
                                   kv[:, db:].reshape(dec_batch, dec_seq, N_HEADS, dh),
                                   gp, gs)):
            lst.append(val)
    return (xp.reshape(batch, seq, d), xs.reshape(dec_batch, dec_seq, d),
            *[jnp.stack(lst) for lst in outs])
```

```python
import functools
import math

import jax
import jax.numpy as jnp
from jax import lax
from jax.experimental import pallas as pl
from jax.experimental.pallas import tpu as pltpu

CHUNK = 64
GMLP_CHUNK = 128
GMLP_GROUPS = 8
N_HEADS = 16
PAST_CHUNKS = 8
WINDOW = PAST_CHUNKS * CHUNK
REL_CLIP = 128
EPS = 1e-6
NEG_INF = -1e30

LANES = 128
Q_BLOCK = 256
KEY_WIN = WINDOW + Q_BLOCK
VMEM_LIMIT_BYTES = 56 * 1024 * 1024
FEW_ROWS = 512
FEW_ROWS_TN = 256

F32 = jnp.float32
BF16 = jnp.bfloat16


def _params(n_axes):
    return pltpu.CompilerParams(dimension_semantics=("arbitrary",) * n_axes,
                                vmem_limit_bytes=VMEM_LIMIT_BYTES)


def _col_pref(m, pref):
    return FEW_ROWS_TN if m <= FEW_ROWS else pref


def _tile(n, pref):
    if n <= pref:
        return n
    t = (pref // LANES) * LANES
    while n % t:
        t -= LANES
    return t


def _rmsnorm_kernel(x_ref, g_ref, o_ref):
    x = x_ref[...]
    ms = jnp.mean(x * x, axis=-1, keepdims=True)
    o_ref[...] = (x * lax.rsqrt(ms + EPS) * g_ref[...]).astype(o_ref.dtype)


def _rmsnorm_bf16(x, g):
    m, d = x.shape
    tr = _tile(m, 256)
    return pl.pallas_call(
        _rmsnorm_kernel,
        grid=(m // tr,),
        in_specs=[pl.BlockSpec((tr, d), lambda i: (i, 0)),
                  pl.BlockSpec((1, d), lambda i: (0, 0))],
        out_specs=pl.BlockSpec((tr, d), lambda i: (i, 0)),
        out_shape=jax.ShapeDtypeStruct((m, d), BF16),
        compiler_params=_params(1),
        name="pre_rmsnorm",
    )(x, g.reshape(1, d))


def _proj_kernel(*refs, n_w, epilogue, side_layout):
    h = refs[0][...]
    accs = [jnp.dot(h, w[...], preferred_element_type=F32) for w in refs[1:1 + n_w]]
    n_side_in = sum(len(widths) for widths in side_layout)
    side_in = refs[1 + n_w:1 + n_w + n_side_in]
    o_ref = refs[1 + n_w + n_side_in]
    side_out = refs[2 + n_w + n_side_in:]
    o_ref[...] = epilogue(*accs).astype(o_ref.dtype)
    k = 0
    for out_ref, widths in zip(side_out, side_layout):
        off = 0
        for width in widths:
            out_ref[:, off:off + width] = side_in[k][...].astype(BF16)
            off += width
            k += 1


def _col_map(i, j, *, first):
    return (0, first + j)


def _slab_map(i, j, *, nj, col):
    return (i * nj + j, col)


def _project(h, w, col_offsets, n_cols, epilogue, out_dtype, name, tn_pref=1024, side=()):
    m, k = h.shape
    tm = _tile(m, 1024)
    tn = _tile(math.gcd(n_cols, *col_offsets), _col_pref(m, tn_pref))
    n_w = len(col_offsets)
    ni, nj = m // tm, n_cols // tn
    w_specs = [pl.BlockSpec((k, tn), functools.partial(_col_map, first=off // tn))
               for off in col_offsets]
    side_in, side_specs, side_out_specs, side_shapes, layout = [], [], [], [], []
    for group in side:
        rows = group[0][0].shape[0]
        slab = rows // (ni * nj)
        assert slab * ni * nj == rows and slab % 16 == 0
        total = sum(width for _, _, width in group)
        for src, col0, width in group:
            assert col0 % width == 0
            side_in.append(src)
            side_specs.append(pl.BlockSpec((slab, width),
                                           functools.partial(_slab_map, nj=nj, col=col0 // width)))
        side_out_specs.append(pl.BlockSpec((slab, total), functools.partial(_slab_map, nj=nj, col=0)))
        side_shapes.append(jax.ShapeDtypeStruct((rows, total), BF16))
        layout.append(tuple(width for _, _, width in group))
    outs = pl.pallas_call(
        functools.partial(_proj_kernel, n_w=n_w, epilogue=epilogue, side_layout=tuple(layout)),
        grid=(ni, nj),
        in_specs=[pl.BlockSpec((tm, k), lambda i, j: (i, 0))] + w_specs + side_specs,
        out_specs=[pl.BlockSpec((tm, tn), lambda i, j: (i, j))] + side_out_specs,
        out_shape=[jax.ShapeDtypeStruct((m, n_cols), out_dtype)] + side_shapes,
        compiler_params=_params(2),
        name=name,
    )(h, *([w] * n_w), *side_in)
    return outs[0] if not side else outs


def _convert_kernel(*refs):
    *pieces, out_ref = refs
    off = 0
    for piece in pieces:
        out_ref[:, off:off + piece.shape[1]] = piece[...].astype(BF16)
        off += piece.shape[1]


def _convert_bf16(group):
    rows = group[0][0].shape[0]
    slab = _tile(rows, 256)
    total = sum(width for _, _, width in group)
    return pl.pallas_call(
        _convert_kernel,
        grid=(rows // slab,),
        in_specs=[pl.BlockSpec((slab, width), functools.partial(lambda i, col: (i, col), col=col0 // width))
                  for _, col0, width in group],
        out_specs=pl.BlockSpec((slab, total), lambda i: (i, 0)),
        out_shape=jax.ShapeDtypeStruct((rows, total), BF16),
        compiler_params=_params(1),
        name="convert_weights",
    )(*[src for src, _, _ in group])


def _gelu(x):
    return 0.5 * x * (1.0 + lax.erf(x * math.sqrt(0.5)))


def _gmlp_gate(u, z):
    return _gelu(u) * jax.nn.silu(z)


SGU_CHUNKS_PER_STEP = 4


def _sgu_kernel(gv_ref, ug_ref, w_ref, bs_ref, lng_ref, lnb_ref, ya_ref, vn_ref, *, rows, gdim):
    kdim = w_ref.shape[2]
    i = lax.broadcasted_iota(jnp.int32, (rows, kdim), 0)
    j = lax.broadcasted_iota(jnp.int32, (rows, kdim), 1)
    visible = (j // CHUNK) <= (i // CHUNK)
    w = [jnp.where(visible, w_ref[g], 0.0).astype(BF16) for g in range(w_ref.shape[0])]
    for r0 in range(0, gv_ref.shape[0], rows):
        chunk = slice(r0, r0 + rows)
        x = gv_ref[chunk, :]
        xc = x - jnp.mean(x, axis=-1, keepdims=True)
        var = jnp.mean(xc * xc, axis=-1, keepdims=True)
        vn = xc * lax.rsqrt(var + EPS) * lng_ref[...] + lnb_ref[...]
        if r0 + rows == gv_ref.shape[0]:
            vn_ref[0] = vn
        vnb = vn.astype(BF16)
        if kdim > rows:
            vnb = jnp.concatenate([vnb, jnp.zeros((kdim - rows, vnb.shape[1]), BF16)], axis=0)
        for g, wg in enumerate(w):
            sl = slice(g * gdim, (g + 1) * gdim)
            s = jnp.dot(wg, vnb[:, sl], preferred_element_type=F32) + bs_ref[:, g:g + 1]
            ya_ref[chunk, sl] = (ug_ref[chunk, sl].astype(F32) * s).astype(BF16)


def _sgu(gv, ug, w_s, b_s, ln_g, ln_b, batch, rows, keep_last_only):
    m, db = gv.shape
    groups = w_s.shape[0]
    n_chunks = m // (batch * rows)
    per_step = math.gcd(n_chunks, SGU_CHUNKS_PER_STEP) if keep_last_only else 1
    n_steps = n_chunks // per_step
    kdim = max(rows, LANES)
    w = w_s[:, :rows, :rows]
    if kdim > rows:
        w = jnp.pad(w, ((0, 0), (0, 0), (0, kdim - rows)))
    bs_t = b_s[:, :rows].T
    row_map = lambda b, c: (b * n_steps + c, 0)
    full2 = lambda b, c: (0, 0)
    if keep_last_only:
        vn_shape, vn_spec = (batch, rows, db), pl.BlockSpec((1, rows, db), lambda b, c: (b, 0, 0))
    else:
        vn_shape = (batch * n_chunks, rows, db)
        vn_spec = pl.BlockSpec((1, rows, db), lambda b, c: (b * n_steps + c, 0, 0))
    step_rows = per_step * rows
    return pl.pallas_call(
        functools.partial(_sgu_kernel, rows=rows, gdim=db // groups),
        grid=(batch, n_steps),
        in_specs=[pl.BlockSpec((step_rows, db), row_map),
                  pl.BlockSpec((step_rows, db), row_map),
                  pl.BlockSpec((groups, rows, kdim), lambda b, c: (0, 0, 0)),
                  pl.BlockSpec((rows, groups), full2),
                  pl.BlockSpec((1, db), full2),
                  pl.BlockSpec((1, db), full2)],
        out_specs=[pl.BlockSpec((step_rows, db), row_map), vn_spec],
        out_shape=[jax.ShapeDtypeStruct((m, db), BF16), jax.ShapeDtypeStruct(vn_shape, F32)],
        compiler_params=_params(2),
        name="sgu",
    )(gv, ug, w, bs_t, ln_g.reshape(1, db), ln_b.reshape(1, db))


def _bias_kernel(rb_ref, t_ref):
    wide = KEY_WIN + Q_BLOCK
    n_tab = rb_ref.shape[1]
    pos = lax.broadcasted_iota(jnp.int32, (n_tab, wide), 1)
    tab = lax.broadcasted_iota(jnp.int32, (n_tab, wide), 0)
    idx = jnp.clip(KEY_WIN - pos, -REL_CLIP, REL_CLIP) + REL_CLIP
    base = jnp.sum(jnp.where(tab == idx, rb_ref[0], 0.0), axis=0, keepdims=True)
    rolled = pltpu.roll(jnp.broadcast_to(base, (Q_BLOCK, wide)), 0, 1, stride=1, stride_axis=0)
    t = rolled[:, Q_BLOCK:]
    qc = lax.broadcasted_iota(jnp.int32, (Q_BLOCK, KEY_WIN), 0) // CHUNK
    kc = lax.broadcasted_iota(jnp.int32, (Q_BLOCK, KEY_WIN), 1) // CHUNK
    t_ref[0] = jnp.where((qc <= kc) & (kc <= qc + PAST_CHUNKS), t, NEG_INF)


def _bias_table(rel_bias):
    heads, n_rel = rel_bias.shape
    n_tab = -(-n_rel // 8) * 8
    rb = jnp.pad(rel_bias, ((0, 0), (0, n_tab - n_rel))).reshape(heads, n_tab, 1)
    return pl.pallas_call(
        _bias_kernel,
        grid=(heads,),
        in_specs=[pl.BlockSpec((1, n_tab, 1), lambda h: (h, 0, 0))],
        out_specs=pl.BlockSpec((1, Q_BLOCK, KEY_WIN), lambda h: (h, 0, 0)),
        out_shape=jax.ShapeDtypeStruct((heads, Q_BLOCK, KEY_WIN), F32),
        compiler_params=_params(1),
        name="attn_bias_table",
    )(rb)


def _softmax_pv(s, v):
    m = jnp.max(s, axis=-1, keepdims=True)
    p = jnp.exp(s - m)
    l = jnp.sum(p, axis=-1, keepdims=True)
    return jnp.dot(p.astype(BF16), v, preferred_element_type=F32) / l


_NT = (((1,), (1,)), ((), ()))


def _attn_prompt_kernel(q_ref, k_ref, v_ref, zb_ref, t_ref, y_ref, nk_ref, nv_ref, kb_s, vb_s,
                        *, seq, keep):
    kb_s[...] = k_ref[0].astype(BF16)
    vb_s[...] = v_ref[0].astype(BF16)
    nk_ref[0] = k_ref[0, seq - keep:, :]
    nv_ref[0] = v_ref[0, seq - keep:, :]

    for qs in range(0, seq, Q_BLOCK):
        ks = max(0, qs - WINDOW)
        rows, keys = slice(qs, qs + Q_BLOCK), slice(ks, qs + Q_BLOCK)
        bias = t_ref[0, :, KEY_WIN - (keys.stop - keys.start):]
        s = lax.dot_general(q_ref[0, rows, :], kb_s[keys, :], _NT,
                            preferred_element_type=F32) + bias
        o = _softmax_pv(s, vb_s[keys, :])
        y_ref[0, rows, :] = (o * zb_ref[0, rows, :].astype(F32)).astype(BF16)


def _attn_prompt(q, kv, zb, table, batch, seq):
    db = q.shape[1]
    dh = db // N_HEADS
    keep = min(WINDOW, seq)
    assert seq % Q_BLOCK == 0 and dh % LANES == 0
    shape3 = (batch, seq, db)
    kv3 = kv.reshape(batch, seq, 2 * db)
    head = pl.BlockSpec((1, seq, dh), lambda b, h: (b, 0, h))
    v_head = pl.BlockSpec((1, seq, dh), lambda b, h: (b, 0, N_HEADS + h))
    kept = pl.BlockSpec((1, keep, dh), lambda b, h: (b, 0, h))
    return pl.pallas_call(
        functools.partial(_attn_prompt_kernel, seq=seq, keep=keep),
        grid=(batch, N_HEADS),
        in_specs=[head, head, v_head, head,
                  pl.BlockSpec((1, Q_BLOCK, KEY_WIN), lambda b, h: (h, 0, 0))],
        out_specs=[head, kept, kept],
        out_shape=[jax.ShapeDtypeStruct(shape3, BF16),
                   jax.ShapeDtypeStruct((batch, keep, db), F32),
                   jax.ShapeDtypeStruct((batch, keep, db), F32)],
        scratch_shapes=[pltpu.VMEM((seq, dh), BF16), pltpu.VMEM((seq, dh), BF16)],
        compiler_params=_params(2),
        name="attn_prompt",
    )(q.reshape(shape3), kv3, kv3, zb.reshape(shape3), table)


HEAD_GROUP = 8


def _attn_sample_kernel(q_ref, kn_ref, vn_ref, zb_ref, kc_ref, vc_ref, b1_ref, b2_ref, y_ref):
    dh = q_ref.shape[-1]
    flat = lambda v: v.reshape(-1, dh)
    q = flat(q_ref[0]).astype(BF16)
    s_c = lax.dot_general(q, flat(kc_ref[0, 0]).astype(BF16), _NT,
                          preferred_element_type=F32) + b1_ref[0]
    s_n = lax.dot_general(q, flat(kn_ref[0]).astype(BF16), _NT,
                          preferred_element_type=F32) + b2_ref[0]
    m = jnp.maximum(jnp.max(s_c, axis=-1, keepdims=True), jnp.max(s_n, axis=-1, keepdims=True))
    p_c = jnp.exp(s_c - m)
    p_n = jnp.exp(s_n - m)
    l = jnp.sum(p_c, axis=-1, keepdims=True) + jnp.sum(p_n, axis=-1, keepdims=True)
    o = (jnp.dot(p_c.astype(BF16), flat(vc_ref[0, 0]).astype(BF16), preferred_element_type=F32)
         + jnp.dot(p_n.astype(BF16), flat(vn_ref[0]).astype(BF16), preferred_element_type=F32))
    y_ref[0] = (o / l * flat(zb_ref[0])).reshape(y_ref.shape[1:])


def _sample_bias_kernel(a_ref, b_ref):
    rows = a_ref.shape[1]
    pos = lax.broadcasted_iota(jnp.int32, (LANES, LANES * HEAD_GROUP), 0)
    col = lax.broadcasted_iota(jnp.int32, (LANES, LANES * HEAD_GROUP), 1)
    expand = (col // HEAD_GROUP == pos).astype(F32)
    r = lax.broadcasted_iota(jnp.int32, (rows, LANES * HEAD_GROUP), 0)
    c = lax.broadcasted_iota(jnp.int32, (rows, LANES * HEAD_GROUP), 1)
    same_head = (r % HEAD_GROUP) == (c % HEAD_GROUP)
    for blk in range(a_ref.shape[2] // LANES):
        wide = jnp.dot(a_ref[0, :, blk * LANES:(blk + 1) * LANES], expand,
                       preferred_element_type=F32, precision=lax.Precision.HIGHEST)
        b_ref[0, :, blk * LANES * HEAD_GROUP:(blk + 1) * LANES * HEAD_GROUP] = jnp.where(
            same_head, wide, NEG_INF)


def _sample_bias(table, n_new, n_pos):
    groups = N_HEADS // HEAD_GROUP
    rows = n_new * HEAD_GROUP
    n_pad = -(-n_pos // LANES) * LANES
    a = table[:, :n_new, :n_pos].reshape(groups, HEAD_GROUP, n_new, n_pos).transpose(0, 2, 1, 3)
    a = jnp.pad(a.reshape(groups, rows, n_pos), ((0, 0), (0, 0), (0, n_pad - n_pos)))
    return pl.pallas_call(
        _sample_bias_kernel,
        grid=(groups,),
        in_specs=[pl.BlockSpec((1, rows, n_pad), lambda g: (g, 0, 0))],
        out_specs=pl.BlockSpec((1, rows, n_pad * HEAD_GROUP), lambda g: (g, 0, 0)),
        out_shape=jax.ShapeDtypeStruct((groups, rows, n_pad * HEAD_GROUP), F32),
        compiler_params=_params(1),
        name="attn_sample_bias",
    )(a)


def _attn_sample(q, kv, zb, k_cache, v_cache, layer, table, batch, n_new):
    db = q.shape[1]
    dh = db // N_HEADS
    n_cache = k_cache.shape[2]
    groups = N_HEADS // HEAD_GROUP
    assert n_cache == WINDOW and n_new <= CHUNK and N_HEADS % HEAD_GROUP == 0
    rows_new, rows_cache = n_new * HEAD_GROUP, n_cache * HEAD_GROUP
    bias = _sample_bias(table, n_new, n_cache + n_new)

    shape4 = (batch, n_new, N_HEADS, dh)
    new = pl.BlockSpec((1, n_new, HEAD_GROUP, dh), lambda g, b: (b, 0, g, 0))
    new_v = pl.BlockSpec((1, n_new, HEAD_GROUP, dh), lambda g, b: (b, 0, groups + g, 0))
    cache = pl.BlockSpec((1, 1, n_cache, HEAD_GROUP, dh), lambda g, b: (layer, b, 0, g, 0))
    kv4 = kv.reshape(batch, n_new, 2 * N_HEADS, dh)
    y = pl.pallas_call(
        _attn_sample_kernel,
        grid=(groups, batch),
        in_specs=[new, new, new_v, new, cache, cache,
                  pl.BlockSpec((1, rows_new, rows_cache), lambda g, b: (g, 0, 0)),
                  pl.BlockSpec((1, rows_new, rows_new), lambda g, b: (g, 0, rows_cache // rows_new))],
        out_specs=new,
        out_shape=jax.ShapeDtypeStruct(shape4, F32),
        compiler_params=_params(2),
        name="attn_sample",
    )(q.astype(F32).reshape(shape4), kv4, kv4, zb.astype(F32).reshape(shape4), k_cache, v_cache,
      bias, bias)
    return y.reshape(batch * n_new, db).astype(BF16)


def _merge_kernel(ya_ref, yb_ref, wa_ref, wb_ref, ga_ref, gb_ref, m_ref):
    a = jnp.dot(ya_ref[...], wa_ref[...], preferred_element_type=F32)
    b = jnp.dot(yb_ref[...], wb_ref[...], preferred_element_type=F32)
    m_ref[...] = (ga_ref[...].astype(F32) * a + gb_ref[...].astype(F32) * b).astype(BF16)


def _merge(ya, yb, w_up, gates):
    m, kb = ya.shape
    n = w_up.shape[1] // 2
    tm, tn = _tile(m, 1024), _tile(n, _col_pref(m, 1024))
    nj = n // tn
    lhs = pl.BlockSpec((tm, kb), lambda i, j: (i, 0))
    rhs_a = pl.BlockSpec((kb, tn), lambda i, j: (0, j))
    rhs_b = pl.BlockSpec((kb, tn), lambda i, j: (0, nj + j))
    return pl.pallas_call(
        _merge_kernel,
        grid=(m // tm, nj),
        in_specs=[lhs, lhs, rhs_a, rhs_b,
                  pl.BlockSpec((tm, tn), lambda i, j: (i, j)),
                  pl.BlockSpec((tm, tn), lambda i, j: (i, nj + j))],
        out_specs=pl.BlockSpec((tm, tn), lambda i, j: (i, j)),
        out_shape=jax.ShapeDtypeStruct((m, n), BF16),
        compiler_params=_params(2),
        name="merge",
    )(ya, yb, w_up, w_up, gates, gates)


def _tail_kernel(m_ref, w_ref, p_ref, wp_ref, g_ref, x_hbm, y_ref, o_s, x1b_s, ssq_s, xbuf, sem,
                 *, nj, tm, tn):
    i = pl.program_id(0)
    j = pl.program_id(1)
    x_copy = pltpu.make_async_copy(x_hbm.at[pl.ds(i * tm, tm), :], xbuf, sem)

    @pl.when(j == 0)
    def _():
        x_copy.start()
        ssq_s[...] = jnp.zeros_like(ssq_s)

    @pl.when(j < nj)
    def _():
        t = jnp.dot(m_ref[...], w_ref[...], preferred_element_type=F32)
        o_s[j] = t
        ssq_s[...] += jnp.sum(t * t, axis=-1, keepdims=True)

    @pl.when(j == nj)
    def _():
        x_copy.wait()
        rstd = lax.rsqrt(ssq_s[...] / (nj * tn) + EPS)
        for jb in range(nj):
            cols = slice(jb * tn, (jb + 1) * tn)
            x1 = xbuf[:, cols] + o_s[jb] * rstd * g_ref[:, cols]
            o_s[jb] = x1
            x1b_s[:, cols] = x1.astype(BF16)

    @pl.when(j >= nj)
    def _():
        gate = jax.nn.sigmoid(jnp.dot(x1b_s[...], w_ref[...], preferred_element_type=F32))
        proj = jnp.dot(p_ref[...].astype(BF16), wp_ref[...], preferred_element_type=F32)
        y_ref[...] = o_s[j - nj] + gate * proj


def _tail(m_act, x, p, w_cat, w_pp, g):
    m, d = x.shape
    pd = p.shape[1]
    tm, tn = _tile(m, 512), _tile(d, _col_pref(m, 1024))
    nj = d // tn
    second = lambda i, j: (0, jnp.maximum(j - nj, 0))
    return pl.pallas_call(
        functools.partial(_tail_kernel, nj=nj, tm=tm, tn=tn),
        grid=(m // tm, 2 * nj),
        in_specs=[pl.BlockSpec((tm, d), lambda i, j: (i, 0)),
                  pl.BlockSpec((d, tn), lambda i, j: (0, j)),
                  pl.BlockSpec((tm, pd), lambda i, j: (i, 0)),
                  pl.BlockSpec((pd, tn), second),
                  pl.BlockSpec((1, d), lambda i, j: (0, 0)),
                  pl.BlockSpec(memory_space=pl.ANY)],
        out_specs=pl.BlockSpec((tm, tn), lambda i, j: (i, jnp.maximum(j - nj, 0))),
        out_shape=jax.ShapeDtypeStruct((m, d), F32),
        scratch_shapes=[pltpu.VMEM((nj, tm, tn), F32),
                        pltpu.VMEM((tm, d), BF16),
                        pltpu.VMEM((tm, 1), F32),
                        pltpu.VMEM((tm, d), F32),
                        pltpu.SemaphoreType.DMA(())],
        compiler_params=_params(2),
        name="tail",
    )(m_act, w_cat, p, w_pp, g.reshape(1, d), x)


def _layer_common(x, pre_g, wb, pending):
    d = x.shape[1]
    db = d // 2
    h = _rmsnorm_bf16(x, pre_g)
    scale = (db // N_HEADS) ** -0.5

    def run(wname, offs, n_cols, epilogue, out_dtype, name, then=(), tn_pref=1024):
        todo = [n for n in then if n in pending]
        outs = _project(h, wb[wname], offs, n_cols, epilogue, out_dtype, name, tn_pref,
                        side=[pending.pop(n) for n in todo])
        if not todo:
            return outs
        wb.update(zip(todo, outs[1:]))
        return outs[0]

    ug = run("gate", (0, db), db, _gmlp_gate, BF16, "proj_gmlp_gate", then=("v",), tn_pref=512)
    gv = run("v", (0,), db, _gelu, F32, "proj_gmlp_v", then=("q",))
    q = run("q", (0,), db, lambda a: a * scale, BF16, "proj_q", then=("kv", "zb"))
    kv = run("kv", (0,), 2 * db, lambda a: a, F32, "proj_kv", then=("gates",))
    zb = run("zb", (0,), db, jax.nn.silu, BF16, "proj_zb")
    gates = run("gates", (0,), 2 * d, jax.nn.sigmoid, BF16, "proj_merge_gates", then=("up", "tail"))
    return ug, gv, q, kv, zb, gates


def _layer_tail(x, p, ya, yb, gates, wb, post_g, w_pp):
    m = _merge(ya, yb, wb["up"], gates)
    return _tail(m, x, p, wb["tail"], w_pp, post_g)


def kernel(x_prompt, x_sample, cache_attn_k, cache_attn_v, p_prompt, p_sample, norm_pre_g, norm_post_g, w_in, gmlp_ln_g, gmlp_ln_b, gmlp_w_s, gmlp_b_s, attn_rel_bias, w_up_a, w_up_b, w_out, w_ple_gate, w_ple_proj):
    batch, seq, d = x_prompt.shape
    dec_batch, dec_seq, _ = x_sample.shape
    db = d // 2
    dh = db // N_HEADS
    depth = w_in.shape[0]
    assert seq % GMLP_CHUNK == 0 and dec_seq <= CHUNK

    xp = x_prompt.reshape(batch * seq, d)
    xs = x_sample.reshape(dec_batch * dec_seq, d)
    outs = [[] for _ in range(6)]
    for i in range(depth):
        wi = w_in[i]
        seg = lambda first, n=1: (wi, first * db, n * db)
        wb = {"gate": _convert_bf16([seg(0), seg(2)])}
        pending = {"v": [seg(1)], "q": [seg(3)], "kv": [seg(4, 2)], "zb": [seg(6)],
                   "gates": [seg(7), seg(8), seg(9), seg(10)],
                   "up": [(w_up_a[i], 0, d), (w_up_b[i], 0, d)],
                   "tail": [(w_out[i], 0, d), (w_ple_gate[i], 0, d)]}
        w_pp = w_ple_proj[i].astype(BF16)
        table = _bias_table(attn_rel_bias[i])
        sgu_w = (gmlp_w_s[i], gmlp_b_s[i], gmlp_ln_g[i], gmlp_ln_b[i])

        ug, gv, q, kv, zb, gates = _layer_common(xp, norm_pre_g[i], wb, pending)
        ya, gp = _sgu(gv, ug, *sgu_w, batch=batch, rows=GMLP_CHUNK, keep_last_only=True)
        yb, kp, vp = _attn_prompt(q, kv, zb, table, batch, seq)
        xp = _layer_tail(xp, p_prompt[i].reshape(batch * seq, -1), ya, yb.reshape(batch * seq, db),
                         gates, wb, norm_post_g[i], w_pp)

        ug, gv, q, kv, zb, gates = _layer_common(xs, norm_pre_g[i], wb, pending)
        ya, gs = _sgu(gv, ug, *sgu_w, batch=dec_batch, rows=dec_seq, keep_last_only=False)
        yb = _attn_sample(q, kv, zb, cache_attn_k, cache_attn_v, i, table, dec_batch, dec_seq)
        xs = _layer_tail(xs, p_sample[i].reshape(dec_batch * dec_seq, -1), ya, yb, gates, wb,
                         norm_post_g[i], w_pp)

        keep = kp.shape[1]
        for lst, val in zip(outs, (kp.reshape(batch, keep, N_HEADS, dh),
                                   vp.reshape(batch, keep, N_HEADS, dh),
                                   kv[:, :db].reshape(dec_batch, dec_seq, N_HEADS, dh),
                                   kv[:, db:].reshape(dec_batch, dec_seq, N_HEADS, dh),
                                   gp, gs)):
            lst.append(val)
    return (xp.reshape(batch, seq, d), xs.reshape(dec_batch, dec_seq, d),
            *[jnp.stack(lst) for lst in outs])
```

```python
import functools
import math

import jax
import jax.numpy as jnp
from jax import lax
from jax.experimental import pallas as pl
from jax.experimental.pallas import tpu as pltpu

CHUNK = 64
GMLP_CHUNK = 128
GMLP_GROUPS = 8
N_HEADS = 16
PAST_CHUNKS = 8
WINDOW = PAST_CHUNKS * CHUNK
REL_CLIP = 128
EPS = 1e-6
NEG_INF = -1e30

LANES = 128
Q_BLOCK = 256
KEY_WIN = WINDOW + Q_BLOCK
VMEM_LIMIT_BYTES = 56 * 1024 * 1024

F32 = jnp.float32
BF16 = jnp.bfloat16


def _params(n_axes):
    return pltpu.CompilerParams(dimension_semantics=("arbitrary",) * n_axes,
                                vmem_limit_bytes=VMEM_LIMIT_BYTES)


def _tile(n, pref):
    if n <= pref:
        return n
    t = (pref // LANES) * LANES
    while n % t:
        t -= LANES
    return t


def _proj_kernel(*refs, n_w, epilogue, side_layout, norm):
    n_side_in = sum(len(widths) for widths in side_layout)
    n_in = (2 if norm else 1) + n_w + n_side_in
    w_refs = refs[n_in - n_side_in - n_w:n_in - n_side_in]
    side_in = refs[n_in - n_side_in:n_in]
    o_ref = refs[n_in]
    side_out = refs[n_in + (2 if norm else 1):]
    if norm:
        x_ref, g_ref, h_ref = refs[0], refs[1], refs[n_in + 1]

        @pl.when(pl.program_id(1) == 0)
        def _():
            x = x_ref[...]
            ms = jnp.mean(x * x, axis=-1, keepdims=True)
            h_ref[...] = (x * lax.rsqrt(ms + EPS) * g_ref[...]).astype(BF16)
    else:
        h_ref = refs[0]
    h = h_ref[...]
    accs = [jnp.dot(h, w[...], preferred_element_type=F32) for w in w_refs]
    o_ref[...] = epilogue(*accs).astype(o_ref.dtype)
    k = 0
    for out_ref, widths in zip(side_out, side_layout):
        off = 0
        for width in widths:
            out_ref[:, off:off + width] = side_in[k][...].astype(BF16)
            off += width
            k += 1


def _col_map(i, j, *, first):
    return (0, first + j)


def _slab_map(i, j, *, nj, col):
    return (i * nj + j, col)


def _project(h, w, col_offsets, n_cols, epilogue, out_dtype, name, tn_pref=1024, side=(),
             norm_gain=None):
    m, k = h.shape
    norm = norm_gain is not None
    tm = _tile(m, 512 if norm else 1024)
    tn = _tile(math.gcd(n_cols, *col_offsets), tn_pref)
    n_w = len(col_offsets)
    ni, nj = m // tm, n_cols // tn
    rows_spec = pl.BlockSpec((tm, k), lambda i, j: (i, 0))
    w_specs = [pl.BlockSpec((k, tn), functools.partial(_col_map, first=off // tn))
               for off in col_offsets]
    side_in, side_specs, side_out_specs, side_shapes, layout = [], [], [], [], []
    for group in side:
        rows = group[0][0].shape[0]
        slab = rows // (ni * nj)
        assert slab * ni * nj == rows and slab % 16 == 0
        total = sum(width for _, _, width in group)
        for src, col0, width in group:
            assert col0 % width == 0
            side_in.append(src)
            side_specs.append(pl.BlockSpec((slab, width),
                                           functools.partial(_slab_map, nj=nj, col=col0 // width)))
        side_out_specs.append(pl.BlockSpec((slab, total), functools.partial(_slab_map, nj=nj, col=0)))
        side_shapes.append(jax.ShapeDtypeStruct((rows, total), BF16))
        layout.append(tuple(width for _, _, width in group))
    lhs = [h, norm_gain.reshape(1, k)] if norm else [h]
    lhs_specs = [rows_spec, pl.BlockSpec((1, k), lambda i, j: (0, 0))] if norm else [rows_spec]
    outs = pl.pallas_call(
        functools.partial(_proj_kernel, n_w=n_w, epilogue=epilogue, side_layout=tuple(layout),
                          norm=norm),
        grid=(ni, nj),
        in_specs=lhs_specs + w_specs + side_specs,
        out_specs=([pl.BlockSpec((tm, tn), lambda i, j: (i, j))] + ([rows_spec] if norm else [])
                   + side_out_specs),
        out_shape=([jax.ShapeDtypeStruct((m, n_cols), out_dtype)]
                   + ([jax.ShapeDtypeStruct((m, k), BF16)] if norm else []) + side_shapes),
        compiler_params=_params(2),
        name=name,
    )(*lhs, *([w] * n_w), *side_in)
    return outs[0] if len(outs) == 1 else outs


def _convert_kernel(*refs):
    *pieces, out_ref = refs
    off = 0
    for piece in pieces:
        out_ref[:, off:off + piece.shape[1]] = piece[...].astype(BF16)
        off += piece.shape[1]


def _convert_bf16(group):
    rows = group[0][0].shape[0]
    slab = _tile(rows, 256)
    total = sum(width for _, _, width in group)
    return pl.pallas_call(
        _convert_kernel,
        grid=(rows // slab,),
        in_specs=[pl.BlockSpec((slab, width), functools.partial(lambda i, col: (i, col), col=col0 // width))
                  for _, col0, width in group],
        out_specs=pl.BlockSpec((slab, total), lambda i: (i, 0)),
        out_shape=jax.ShapeDtypeStruct((rows, total), BF16),
        compiler_params=_params(1),
        name="convert_weights",
    )(*[src for src, _, _ in group])


def _gelu(x):
    return 0.5 * x * (1.0 + lax.erf(x * math.sqrt(0.5)))


def _gmlp_gate(u, z):
    return _gelu(u) * jax.nn.silu(z)


SGU_CHUNKS_PER_STEP = 4


def _sgu_kernel(gv_ref, ug_ref, w_ref, bs_ref, lng_ref, lnb_ref, ya_ref, vn_ref, *, rows, gdim):
    kdim = w_ref.shape[2]
    i = lax.broadcasted_iota(jnp.int32, (rows, kdim), 0)
    j = lax.broadcasted_iota(jnp.int32, (rows, kdim), 1)
    visible = (j // CHUNK) <= (i // CHUNK)
    w = [jnp.where(visible, w_ref[g], 0.0).astype(BF16) for g in range(w_ref.shape[0])]
    for r0 in range(0, gv_ref.shape[0], rows):
        chunk = slice(r0, r0 + rows)
        x = gv_ref[chunk, :]
        xc = x - jnp.mean(x, axis=-1, keepdims=True)
        var = jnp.mean(xc * xc, axis=-1, keepdims=True)
        vn = xc * lax.rsqrt(var + EPS) * lng_ref[...] + lnb_ref[...]
        if r0 + rows == gv_ref.shape[0]:
            vn_ref[0] = vn
        vnb = vn.astype(BF16)
        if kdim > rows:
            vnb = jnp.concatenate([vnb, jnp.zeros((kdim - rows, vnb.shape[1]), BF16)], axis=0)
        for g, wg in enumerate(w):
            sl = slice(g * gdim, (g + 1) * gdim)
            s = jnp.dot(wg, vnb[:, sl], preferred_element_type=F32) + bs_ref[:, g:g + 1]
            ya_ref[chunk, sl] = (ug_ref[chunk, sl].astype(F32) * s).astype(BF16)


def _sgu(gv, ug, w_s, b_s, ln_g, ln_b, batch, rows, keep_last_only):
    m, db = gv.shape
    groups = w_s.shape[0]
    n_chunks = m // (batch * rows)
    per_step = math.gcd(n_chunks, SGU_CHUNKS_PER_STEP) if keep_last_only else 1
    n_steps = n_chunks // per_step
    kdim = max(rows, LANES)
    w = w_s[:, :rows, :rows]
    if kdim > rows:
        w = jnp.pad(w, ((0, 0), (0, 0), (0, kdim - rows)))
    bs_t = b_s[:, :rows].T
    row_map = lambda b, c: (b * n_steps + c, 0)
    full2 = lambda b, c: (0, 0)
    if keep_last_only:
        vn_shape, vn_spec = (batch, rows, db), pl.BlockSpec((1, rows, db), lambda b, c: (b, 0, 0))
    else:
        vn_shape = (batch * n_chunks, rows, db)
        vn_spec = pl.BlockSpec((1, rows, db), lambda b, c: (b * n_steps + c, 0, 0))
    step_rows = per_step * rows
    return pl.pallas_call(
        functools.partial(_sgu_kernel, rows=rows, gdim=db // groups),
        grid=(batch, n_steps),
        in_specs=[pl.BlockSpec((step_rows, db), row_map),
                  pl.BlockSpec((step_rows, db), row_map),
                  pl.BlockSpec((groups, rows, kdim), lambda b, c: (0, 0, 0)),
                  pl.BlockSpec((rows, groups), full2),
                  pl.BlockSpec((1, db), full2),
                  pl.BlockSpec((1, db), full2)],
        out_specs=[pl.BlockSpec((step_rows, db), row_map), vn_spec],
        out_shape=[jax.ShapeDtypeStruct((m, db), BF16), jax.ShapeDtypeStruct(vn_shape, F32)],
        compiler_params=_params(2),
        name="sgu",
    )(gv, ug, w, bs_t, ln_g.reshape(1, db), ln_b.reshape(1, db))


def _bias_kernel(rb_ref, t_ref):
    wide = KEY_WIN + Q_BLOCK
    n_tab = rb_ref.shape[1]
    pos = lax.broadcasted_iota(jnp.int32, (n_tab, wide), 1)
    tab = lax.broadcasted_iota(jnp.int32, (n_tab, wide), 0)
    idx = jnp.clip(KEY_WIN - pos, -REL_CLIP, REL_CLIP) + REL_CLIP
    base = jnp.sum(jnp.where(tab == idx, rb_ref[0], 0.0), axis=0, keepdims=True)
    rolled = pltpu.roll(jnp.broadcast_to(base, (Q_BLOCK, wide)), 0, 1, stride=1, stride_axis=0)
    t = rolled[:, Q_BLOCK:]
    qc = lax.broadcasted_iota(jnp.int32, (Q_BLOCK, KEY_WIN), 0) // CHUNK
    kc = lax.broadcasted_iota(jnp.int32, (Q_BLOCK, KEY_WIN), 1) // CHUNK
    t_ref[0] = jnp.where((qc <= kc) & (kc <= qc + PAST_CHUNKS), t, NEG_INF)


def _bias_table(rel_bias):
    heads, n_rel = rel_bias.shape
    n_tab = -(-n_rel // 8) * 8
    rb = jnp.pad(rel_bias, ((0, 0), (0, n_tab - n_rel))).reshape(heads, n_tab, 1)
    return pl.pallas_call(
        _bias_kernel,
        grid=(heads,),
        in_specs=[pl.BlockSpec((1, n_tab, 1), lambda h: (h, 0, 0))],
        out_specs=pl.BlockSpec((1, Q_BLOCK, KEY_WIN), lambda h: (h, 0, 0)),
        out_shape=jax.ShapeDtypeStruct((heads, Q_BLOCK, KEY_WIN), F32),
        compiler_params=_params(1),
        name="attn_bias_table",
    )(rb)


def _softmax_pv(s, v):
    m = jnp.max(s, axis=-1, keepdims=True)
    p = jnp.exp(s - m)
    l = jnp.sum(p, axis=-1, keepdims=True)
    return jnp.dot(p.astype(BF16), v, preferred_element_type=F32) / l


_NT = (((1,), (1,)), ((), ()))


def _attn_prompt_kernel(q_ref, k_ref, v_ref, zb_ref, t_ref, y_ref, nk_ref, nv_ref, kb_s, vb_s,
                        *, seq, keep):
    kb_s[...] = k_ref[0].astype(BF16)
    vb_s[...] = v_ref[0].astype(BF16)
    nk_ref[0] = k_ref[0, seq - keep:, :]
    nv_ref[0] = v_ref[0, seq - keep:, :]

    for qs in range(0, seq, Q_BLOCK):
        ks = max(0, qs - WINDOW)
        rows, keys = slice(qs, qs + Q_BLOCK), slice(ks, qs + Q_BLOCK)
        bias = t_ref[0, :, KEY_WIN - (keys.stop - keys.start):]
        s = lax.dot_general(q_ref[0, rows, :], kb_s[keys, :], _NT,
                            preferred_element_type=F32) + bias
        o = _softmax_pv(s, vb_s[keys, :])
        y_ref[0, rows, :] = (o * zb_ref[0, rows, :].astype(F32)).astype(BF16)


def _attn_prompt(q, kv, zb, table, batch, seq):
    db = q.shape[1]
    dh = db // N_HEADS
    keep = min(WINDOW, seq)
    assert seq % Q_BLOCK == 0 and dh % LANES == 0
    shape3 = (batch, seq, db)
    kv3 = kv.reshape(batch, seq, 2 * db)
    head = pl.BlockSpec((1, seq, dh), lambda b, h: (b, 0, h))
    v_head = pl.BlockSpec((1, seq, dh), lambda b, h: (b, 0, N_HEADS + h))
    kept = pl.BlockSpec((1, keep, dh), lambda b, h: (b, 0, h))
    return pl.pallas_call(
        functools.partial(_attn_prompt_kernel, seq=seq, keep=keep),
        grid=(batch, N_HEADS),
        in_specs=[head, head, v_head, head,
                  pl.BlockSpec((1, Q_BLOCK, KEY_WIN), lambda b, h: (h, 0, 0))],
        out_specs=[head, kept, kept],
        out_shape=[jax.ShapeDtypeStruct(shape3, BF16),
                   jax.ShapeDtypeStruct((batch, keep, db), F32),
                   jax.ShapeDtypeStruct((batch, keep, db), F32)],
        scratch_shapes=[pltpu.VMEM((seq, dh), BF16), pltpu.VMEM((seq, dh), BF16)],
        compiler_params=_params(2),
        name="attn_prompt",
    )(q.reshape(shape3), kv3, kv3, zb.reshape(shape3), table)


HEAD_GROUP = 8


def _attn_sample_kernel(q_ref, kn_ref, vn_ref, zb_ref, kc_ref, vc_ref, b1_ref, b2_ref, y_ref):
    dh = q_ref.shape[-1]
    flat = lambda v: v.reshape(-1, dh)
    q = flat(q_ref[0]).astype(BF16)
    s_c = lax.dot_general(q, flat(kc_ref[0, 0]).astype(BF16), _NT,
                          preferred_element_type=F32) + b1_ref[0]
    s_n = lax.dot_general(q, flat(kn_ref[0]).astype(BF16), _NT,
                          preferred_element_type=F32) + b2_ref[0]
    m = jnp.maximum(jnp.max(s_c, axis=-1, keepdims=True), jnp.max(s_n, axis=-1, keepdims=True))
    p_c = jnp.exp(s_c - m)
    p_n = jnp.exp(s_n - m)
    l = jnp.sum(p_c, axis=-1, keepdims=True) + jnp.sum(p_n, axis=-1, keepdims=True)
    o = (jnp.dot(p_c.astype(BF16), flat(vc_ref[0, 0]).astype(BF16), preferred_element_type=F32)
         + jnp.dot(p_n.astype(BF16), flat(vn_ref[0]).astype(BF16), preferred_element_type=F32))
    y_ref[0] = (o / l * flat(zb_ref[0])).reshape(y_ref.shape[1:])


def _sample_bias_kernel(a_ref, b_ref):
    rows = a_ref.shape[1]
    pos = lax.broadcasted_iota(jnp.int32, (LANES, LANES * HEAD_GROUP), 0)
    col = lax.broadcasted_iota(jnp.int32, (LANES, LANES * HEAD_GROUP), 1)
    expand = (col // HEAD_GROUP == pos).astype(F32)
    r = lax.broadcasted_iota(jnp.int32, (rows, LANES * HEAD_GROUP), 0)
    c = lax.broadcasted_iota(jnp.int32, (rows, LANES * HEAD_GROUP), 1)
    same_head = (r % HEAD_GROUP) == (c % HEAD_GROUP)
    for blk in range(a_ref.shape[2] // LANES):
        wide = jnp.dot(a_ref[0, :, blk * LANES:(blk + 1) * LANES], expand,
                       preferred_element_type=F32, precision=lax.Precision.HIGHEST)
        b_ref[0, :, blk * LANES * HEAD_GROUP:(blk + 1) * LANES * HEAD_GROUP] = jnp.where(
            same_head, wide, NEG_INF)


def _sample_bias(table, n_new, n_pos):
    groups = N_HEADS // HEAD_GROUP
    rows = n_new * HEAD_GROUP
    n_pad = -(-n_pos // LANES) * LANES
    a = table[:, :n_new, :n_pos].reshape(groups, HEAD_GROUP, n_new, n_pos).transpose(0, 2, 1, 3)
    a = jnp.pad(a.reshape(groups, rows, n_pos), ((0, 0), (0, 0), (0, n_pad - n_pos)))
    return pl.pallas_call(
        _sample_bias_kernel,
        grid=(groups,),
        in_specs=[pl.BlockSpec((1, rows, n_pad), lambda g: (g, 0, 0))],
        out_specs=pl.BlockSpec((1, rows, n_pad * HEAD_GROUP), lambda g: (g, 0, 0)),
        out_shape=jax.ShapeDtypeStruct((groups, rows, n_pad * HEAD_GROUP), F32),
        compiler_params=_params(1),
        name="attn_sample_bias",
    )(a)


def _attn_sample(q, kv, zb, k_cache, v_cache, layer, table, batch, n_new):
    db = q.shape[1]
    dh = db // N_HEADS
    n_cache = k_cache.shape[2]
    groups = N_HEADS // HEAD_GROUP
    assert n_cache == WINDOW and n_new <= CHUNK and N_HEADS % HEAD_GROUP == 0
    rows_new, rows_cache = n_new * HEAD_GROUP, n_cache * HEAD_GROUP
    bias = _sample_bias(table, n_new, n_cache + n_new)

    shape4 = (batch, n_new, N_HEADS, dh)
    new = pl.BlockSpec((1, n_new, HEAD_GROUP, dh), lambda g, b: (b, 0, g, 0))
    new_v = pl.BlockSpec((1, n_new, HEAD_GROUP, dh), lambda g, b: (b, 0, groups + g, 0))
    cache = pl.BlockSpec((1, 1, n_cache, HEAD_GROUP, dh), lambda g, b: (layer, b, 0, g, 0))
    kv4 = kv.reshape(batch, n_new, 2 * N_HEADS, dh)
    y = pl.pallas_call(
        _attn_sample_kernel,
        grid=(groups, batch),
        in_specs=[new, new, new_v, new, cache, cache,
                  pl.BlockSpec((1, rows_new, rows_cache), lambda g, b: (g, 0, 0)),
                  pl.BlockSpec((1, rows_new, rows_new), lambda g, b: (g, 0, rows_cache // rows_new))],
        out_specs=new,
        out_shape=jax.ShapeDtypeStruct(shape4, F32),
        compiler_params=_params(2),
        name="attn_sample",
    )(q.astype(F32).reshape(shape4), kv4, kv4, zb.astype(F32).reshape(shape4), k_cache, v_cache,
      bias, bias)
    return y.reshape(batch * n_new, db).astype(BF16)


def _merge_kernel(ya_ref, yb_ref, wa_ref, wb_ref, ga_ref, gb_ref, m_ref):
    a = jnp.dot(ya_ref[...], wa_ref[...], preferred_element_type=F32)
    b = jnp.dot(yb_ref[...], wb_ref[...], preferred_element_type=F32)
    m_ref[...] = (ga_ref[...].astype(F32) * a + gb_ref[...].astype(F32) * b).astype(BF16)


def _merge(ya, yb, w_up, gates):
    m, kb = ya.shape
    n = w_up.shape[1] // 2
    tm, tn = _tile(m, 1024), _tile(n, 1024)
    nj = n // tn
    lhs = pl.BlockSpec((tm, kb), lambda i, j: (i, 0))
    rhs_a = pl.BlockSpec((kb, tn), lambda i, j: (0, j))
    rhs_b = pl.BlockSpec((kb, tn), lambda i, j: (0, nj + j))
    return pl.pallas_call(
        _merge_kernel,
        grid=(m // tm, nj),
        in_specs=[lhs, lhs, rhs_a, rhs_b,
                  pl.BlockSpec((tm, tn), lambda i, j: (i, j)),
                  pl.BlockSpec((tm, tn), lambda i, j: (i, nj + j))],
        out_specs=pl.BlockSpec((tm, tn), lambda i, j: (i, j)),
        out_shape=jax.ShapeDtypeStruct((m, n), BF16),
        compiler_params=_params(2),
        name="merge",
    )(ya, yb, w_up, w_up, gates, gates)


def _tail_kernel(m_ref, w_ref, p_ref, wp_ref, g_ref, x_hbm, y_ref, o_s, x1b_s, ssq_s, xbuf, sem,
                 *, nj, tm, tn):
    i = pl.program_id(0)
    j = pl.program_id(1)
    x_copy = pltpu.make_async_copy(x_hbm.at[pl.ds(i * tm, tm), :], xbuf, sem)

    @pl.when(j == 0)
    def _():
        x_copy.start()
        ssq_s[...] = jnp.zeros_like(ssq_s)

    @pl.when(j < nj)
    def _():
        t = jnp.dot(m_ref[...], w_ref[...], preferred_element_type=F32)
        o_s[j] = t
        ssq_s[...] += jnp.sum(t * t, axis=-1, keepdims=True)

    @pl.when(j == nj)
    def _():
        x_copy.wait()
        rstd = lax.rsqrt(ssq_s[...] / (nj * tn) + EPS)
        for jb in range(nj):
            cols = slice(jb * tn, (jb + 1) * tn)
            x1 = xbuf[:, cols] + o_s[jb] * rstd * g_ref[:, cols]
            o_s[jb] = x1
            x1b_s[:, cols] = x1.astype(BF16)

    @pl.when(j >= nj)
    def _():
        gate = jax.nn.sigmoid(jnp.dot(x1b_s[...], w_ref[...], preferred_element_type=F32))
        proj = jnp.dot(p_ref[...].astype(BF16), wp_ref[...], preferred_element_type=F32)
        y_ref[...] = o_s[j - nj] + gate * proj


def _tail(m_act, x, p, w_cat, w_pp, g):
    m, d = x.shape
    pd = p.shape[1]
    tm, tn = _tile(m, 512), _tile(d, 1024)
    nj = d // tn
    second = lambda i, j: (0, jnp.maximum(j - nj, 0))
    return pl.pallas_call(
        functools.partial(_tail_kernel, nj=nj, tm=tm, tn=tn),
        grid=(m // tm, 2 * nj),
        in_specs=[pl.BlockSpec((tm, d), lambda i, j: (i, 0)),
                  pl.BlockSpec((d, tn), lambda i, j: (0, j)),
                  pl.BlockSpec((tm, pd), lambda i, j: (i, 0)),
                  pl.BlockSpec((pd, tn), second),
                  pl.BlockSpec((1, d), lambda i, j: (0, 0)),
                  pl.BlockSpec(memory_space=pl.ANY)],
        out_specs=pl.BlockSpec((tm, tn), lambda i, j: (i, jnp.maximum(j - nj, 0))),
        out_shape=jax.ShapeDtypeStruct((m, d), F32),
        scratch_shapes=[pltpu.VMEM((nj, tm, tn), F32),
                        pltpu.VMEM((tm, d), BF16),
                        pltpu.VMEM((tm, 1), F32),
                        pltpu.VMEM((tm, d), F32),
                        pltpu.SemaphoreType.DMA(())],
        compiler_params=_params(2),
        name="tail",
    )(m_act, w_cat, p, w_pp, g.reshape(1, d), x)


def _layer_common(x, pre_g, wb, pending):
    d = x.shape[1]
    db = d // 2
    scale = (db // N_HEADS) ** -0.5
    ug, h, *converted = _project(x, wb["gate"], (0, db), db, _gmlp_gate, BF16, "proj_gmlp_gate",
                                 tn_pref=512, side=[pending.pop(n) for n in ("v",) if n in pending],
                                 norm_gain=pre_g)
    wb.update(zip(("v",), converted))

    def run(wname, offs, n_cols, epilogue, out_dtype, name, then=(), tn_pref=1024):
        todo = [n for n in then if n in pending]
        outs = _project(h, wb[wname], offs, n_cols, epilogue, out_dtype, name, tn_pref,
                        side=[pending.pop(n) for n in todo])
        if not todo:
            return outs
        wb.update(zip(todo, outs[1:]))
        return outs[0]

    gv = run("v", (0,), db, _gelu, F32, "proj_gmlp_v", then=("q",))
    q = run("q", (0,), db, lambda a: a * scale, BF16, "proj_q", then=("kv", "zb"))
    kv = run("kv", (0,), 2 * db, lambda a: a, F32, "proj_kv", then=("gates",))
    zb = run("zb", (0,), db, jax.nn.silu, BF16, "proj_zb")
    gates = run("gates", (0,), 2 * d, jax.nn.sigmoid, BF16, "proj_merge_gates", then=("up", "tail"))
    return ug, gv, q, kv, zb, gates


def _layer_tail(x, p, ya, yb, gates, wb, post_g, w_pp):
    m = _merge(ya, yb, wb["up"], gates)
    return _tail(m, x, p, wb["tail"], w_pp, post_g)


def kernel(x_prompt, x_sample, cache_attn_k, cache_attn_v, p_prompt, p_sample, norm_pre_g, norm_post_g, w_in, gmlp_ln_g, gmlp_ln_b, gmlp_w_s, gmlp_b_s, attn_rel_bias, w_up_a, w_up_b, w_out, w_ple_gate, w_ple_proj):
    batch, seq, d = x_prompt.shape
    dec_batch, dec_seq, _ = x_sample.shape
    db = d // 2
    dh = db // N_HEADS
    depth = w_in.shape[0]
    assert seq % GMLP_CHUNK == 0 and dec_seq <= CHUNK

    xp = x_prompt.reshape(batch * seq, d)
    xs = x_sample.reshape(dec_batch * dec_seq, d)
    outs = [[] for _ in range(6)]
    for i in range(depth):
        wi = w_in[i]
        seg = lambda first, n=1: (wi, first * db, n * db)
        wb = {"gate": _convert_bf16([seg(0), seg(2)])}
        pending = {"v": [seg(1)], "q": [seg(3)], "kv": [seg(4, 2)], "zb": [seg(6)],
                   "gates": [seg(7), seg(8), seg(9), seg(10)],
                   "up": [(w_up_a[i], 0, d), (w_up_b[i], 0, d)],
                   "tail": [(w_out[i], 0, d), (w_ple_gate[i], 0, d)]}
        w_pp = w_ple_proj[i].astype(BF16)
        table = _bias_table(attn_rel_bias[i])
        sgu_w = (gmlp_w_s[i], gmlp_b_s[i], gmlp_ln_g[i], gmlp_ln_b[i])

        ug, gv, q, kv, zb, gates = _layer_common(xp, norm_pre_g[i], wb, pending)
        ya, gp = _sgu(gv, ug, *sgu_w, batch=batch, rows=GMLP_CHUNK, keep_last_only=True)
        yb, kp, vp = _attn_prompt(q, kv, zb, table, batch, seq)
        xp = _layer_tail(xp, p_prompt[i].reshape(batch * seq, -1), ya, yb.reshape(batch * seq, db),
                         gates, wb, norm_post_g[i], w_pp)

        ug, gv, q, kv, zb, gates = _layer_common(xs, norm_pre_g[i], wb, pending)
        ya, gs = _sgu(gv, ug, *sgu_w, batch=dec_batch, rows=dec_seq, keep_last_only=False)
        yb = _attn_sample(q, kv, zb, cache_attn_k, cache_attn_v, i, table, dec_batch, dec_seq)
        xs = _layer_tail(xs, p_sample[i].reshape(dec_batch * dec_seq, -1), ya, yb, gates, wb,
                         norm_post_g[i], w_pp)

        keep = kp.shape[1]
        for lst, val in zip(outs, (kp.reshape(batch, keep, N_HEADS, dh),
                                   vp.reshape(batch, keep, N_HEADS, dh),
                                   kv[:, :db].reshape(dec_batch, dec_seq, N_HEADS, dh),
                                   kv[:, db:].reshape(dec_batch, dec_seq, N_HEADS, dh),
                                   gp, gs)):
            lst.append(val)
    return (xp.reshape(batch, seq, d), xs.reshape(dec_batch, dec_seq, d),
            *[jnp.stack(lst) for lst in outs])
```

```python
import functools
import math

import jax
import jax.numpy as jnp
from jax import lax
from jax.experimental import pallas as pl
from jax.experimental.pallas import tpu as pltpu

CHUNK = 64
GMLP_CHUNK = 128
GMLP_GROUPS = 8
N_HEADS = 16
PAST_CHUNKS = 8
WINDOW = PAST_CHUNKS * CHUNK
REL_CLIP = 128
EPS = 1e-6
NEG_INF = -1e30

LANES = 128
Q_BLOCK = 256
KEY_WIN = WINDOW + Q_BLOCK
VMEM_LIMIT_BYTES = 56 * 1024 * 1024

F32 = jnp.float32
BF16 = jnp.bfloat16


def _params(n_axes):
    return pltpu.CompilerParams(dimension_semantics=("arbitrary",) * n_axes,
                                vmem_limit_bytes=VMEM_LIMIT_BYTES)


def _tile(n, pref):
    if n <= pref:
        return n
    t = (pref // LANES) * LANES
    while n % t:
        t -= LANES
    return t


def _rmsnorm_kernel(x_ref, g_ref, o_ref):
    x = x_ref[...]
    ms = jnp.mean(x * x, axis=-1, keepdims=True)
    o_ref[...] = (x * lax.rsqrt(ms + EPS) * g_ref[...]).astype(o_ref.dtype)


def _rmsnorm_bf16(x, g):
    m, d = x.shape
    tr = _tile(m, 256)
    return pl.pallas_call(
        _rmsnorm_kernel,
        grid=(m // tr,),
        in_specs=[pl.BlockSpec((tr, d), lambda i: (i, 0)),
                  pl.BlockSpec((1, d), lambda i: (0, 0))],
        out_specs=pl.BlockSpec((tr, d), lambda i: (i, 0)),
        out_shape=jax.ShapeDtypeStruct((m, d), BF16),
        compiler_params=_params(1),
        name="pre_rmsnorm",
    )(x, g.reshape(1, d))


def _proj_kernel(*refs, n_w, epilogue, side_layout):
    h = refs[0][...]
    accs = [jnp.dot(h, w[...], preferred_element_type=F32) for w in refs[1:1 + n_w]]
    n_side_in = sum(len(widths) for widths in side_layout)
    side_in = refs[1 + n_w:1 + n_w + n_side_in]
    o_ref = refs[1 + n_w + n_side_in]
    side_out = refs[2 + n_w + n_side_in:]
    o_ref[...] = epilogue(*accs).astype(o_ref.dtype)
    k = 0
    for out_ref, widths in zip(side_out, side_layout):
        off = 0
        for width in widths:
            out_ref[:, off:off + width] = side_in[k][...].astype(BF16)
            off += width
            k += 1


def _col_map(i, j, *, first):
    return (0, first + j)


def _slab_map(i, j, *, nj, col):
    return (i * nj + j, col)


def _project(h, w, col_offsets, n_cols, epilogue, out_dtype, name, tn_pref=1024, side=()):
    m, k = h.shape
    tm = _tile(m, 1024)
    tn = _tile(math.gcd(n_cols, *col_offsets), tn_pref)
    n_w = len(col_offsets)
    ni, nj = m // tm, n_cols // tn
    w_specs = [pl.BlockSpec((k, tn), functools.partial(_col_map, first=off // tn))
               for off in col_offsets]
    side_in, side_specs, side_out_specs, side_shapes, layout = [], [], [], [], []
    for group in side:
        rows = group[0][0].shape[0]
        slab = rows // (ni * nj)
        assert slab * ni * nj == rows and slab % 16 == 0
        total = sum(width for _, _, width in group)
        for src, col0, width in group:
            assert col0 % width == 0
            side_in.append(src)
            side_specs.append(pl.BlockSpec((slab, width),
                                           functools.partial(_slab_map, nj=nj, col=col0 // width)))
        side_out_specs.append(pl.BlockSpec((slab, total), functools.partial(_slab_map, nj=nj, col=0)))
        side_shapes.append(jax.ShapeDtypeStruct((rows, total), BF16))
        layout.append(tuple(width for _, _, width in group))
    outs = pl.pallas_call(
        functools.partial(_proj_kernel, n_w=n_w, epilogue=epilogue, side_layout=tuple(layout)),
        grid=(ni, nj),
        in_specs=[pl.BlockSpec((tm, k), lambda i, j: (i, 0))] + w_specs + side_specs,
        out_specs=[pl.BlockSpec((tm, tn), lambda i, j: (i, j))] + side_out_specs,
        out_shape=[jax.ShapeDtypeStruct((m, n_cols), out_dtype)] + side_shapes,
        compiler_params=_params(2),
        name=name,
    )(h, *([w] * n_w), *side_in)
    return outs[0] if not side else outs


def _convert_kernel(*refs):
    *pieces, out_ref = refs
    off = 0
    for piece in pieces:
        out_ref[:, off:off + piece.shape[1]] = piece[...].astype(BF16)
        off += piece.shape[1]


def _convert_bf16(group):
    rows = group[0][0].shape[0]
    slab = _tile(rows, 256)
    total = sum(width for _, _, width in group)
    return pl.pallas_call(
        _convert_kernel,
        grid=(rows // slab,),
        in_specs=[pl.BlockSpec((slab, width), functools.partial(lambda i, col: (i, col), col=col0 // width))
                  for _, col0, width in group],
        out_specs=pl.BlockSpec((slab, total), lambda i: (i, 0)),
        out_shape=jax.ShapeDtypeStruct((rows, total), BF16),
        compiler_params=_params(1),
        name="convert_weights",
    )(*[src for src, _, _ in group])


def _gelu(x):
    return 0.5 * x * (1.0 + lax.erf(x * math.sqrt(0.5)))


def _gmlp_gate(u, z):
    return _gelu(u) * jax.nn.silu(z)


SGU_CHUNKS_PER_STEP = 4


def _sgu_kernel(gv_ref, ug_ref, w_ref, bs_ref, lng_ref, lnb_ref, ya_ref, vn_ref, *, rows, gdim):
    kdim = w_ref.shape[2]
    i = lax.broadcasted_iota(jnp.int32, (rows, kdim), 0)
    j = lax.broadcasted_iota(jnp.int32, (rows, kdim), 1)
    visible = (j // CHUNK) <= (i // CHUNK)
    w = [jnp.where(visible, w_ref[g], 0.0).astype(BF16) for g in range(w_ref.shape[0])]
    for r0 in range(0, gv_ref.shape[0], rows):
        chunk = slice(r0, r0 + rows)
        x = gv_ref[chunk, :]
        xc = x - jnp.mean(x, axis=-1, keepdims=True)
        var = jnp.mean(xc * xc, axis=-1, keepdims=True)
        vn = xc * lax.rsqrt(var + EPS) * lng_ref[...] + lnb_ref[...]
        if r0 + rows == gv_ref.shape[0]:
            vn_ref[0] = vn
        vnb = vn.astype(BF16)
        if kdim > rows:
            vnb = jnp.concatenate([vnb, jnp.zeros((kdim - rows, vnb.shape[1]), BF16)], axis=0)
        for g, wg in enumerate(w):
            sl = slice(g * gdim, (g + 1) * gdim)
            s = jnp.dot(wg, vnb[:, sl], preferred_element_type=F32) + bs_ref[:, g:g + 1]
            ya_ref[chunk, sl] = (ug_ref[chunk, sl].astype(F32) * s).astype(BF16)


def _sgu(gv, ug, w_s, b_s, ln_g, ln_b, batch, rows, keep_last_only):
    m, db = gv.shape
    groups = w_s.shape[0]
    n_chunks = m // (batch * rows)
    per_step = math.gcd(n_chunks, SGU_CHUNKS_PER_STEP) if keep_last_only else 1
    n_steps = n_chunks // per_step
    kdim = max(rows, LANES)
    w = w_s[:, :rows, :rows]
    if kdim > rows:
        w = jnp.pad(w, ((0, 0), (0, 0), (0, kdim - rows)))
    bs_t = b_s[:, :rows].T
    row_map = lambda b, c: (b * n_steps + c, 0)
    full2 = lambda b, c: (0, 0)
    if keep_last_only:
        vn_shape, vn_spec = (batch, rows, db), pl.BlockSpec((1, rows, db), lambda b, c: (b, 0, 0))
    else:
        vn_shape = (batch * n_chunks, rows, db)
        vn_spec = pl.BlockSpec((1, rows, db), lambda b, c: (b * n_steps + c, 0, 0))
    step_rows = per_step * rows
    return pl.pallas_call(
        functools.partial(_sgu_kernel, rows=rows, gdim=db // groups),
        grid=(batch, n_steps),
        in_specs=[pl.BlockSpec((step_rows, db), row_map),
                  pl.BlockSpec((step_rows, db), row_map),
                  pl.BlockSpec((groups, rows, kdim), lambda b, c: (0, 0, 0)),
                  pl.BlockSpec((rows, groups), full2),
                  pl.BlockSpec((1, db), full2),
                  pl.BlockSpec((1, db), full2)],
        out_specs=[pl.BlockSpec((step_rows, db), row_map), vn_spec],
        out_shape=[jax.ShapeDtypeStruct((m, db), BF16), jax.ShapeDtypeStruct(vn_shape, F32)],
        compiler_params=_params(2),
        name="sgu",
    )(gv, ug, w, bs_t, ln_g.reshape(1, db), ln_b.reshape(1, db))


def _bias_kernel(rb_ref, t_ref):
    wide = KEY_WIN + Q_BLOCK
    n_tab = rb_ref.shape[1]
    pos = lax.broadcasted_iota(jnp.int32, (n_tab, wide), 1)
    tab = lax.broadcasted_iota(jnp.int32, (n_tab, wide), 0)
    idx = jnp.clip(KEY_WIN - pos, -REL_CLIP, REL_CLIP) + REL_CLIP
    base = jnp.sum(jnp.where(tab == idx, rb_ref[0], 0.0), axis=0, keepdims=True)
    rolled = pltpu.roll(jnp.broadcast_to(base, (Q_BLOCK, wide)), 0, 1, stride=1, stride_axis=0)
    t = rolled[:, Q_BLOCK:]
    qc = lax.broadcasted_iota(jnp.int32, (Q_BLOCK, KEY_WIN), 0) // CHUNK
    kc = lax.broadcasted_iota(jnp.int32, (Q_BLOCK, KEY_WIN), 1) // CHUNK
    t_ref[0] = jnp.where((qc <= kc) & (kc <= qc + PAST_CHUNKS), t, NEG_INF)


def _bias_table(rel_bias):
    heads, n_rel = rel_bias.shape
    n_tab = -(-n_rel // 8) * 8
    rb = jnp.pad(rel_bias, ((0, 0), (0, n_tab - n_rel))).reshape(heads, n_tab, 1)
    return pl.pallas_call(
        _bias_kernel,
        grid=(heads,),
        in_specs=[pl.BlockSpec((1, n_tab, 1), lambda h: (h, 0, 0))],
        out_specs=pl.BlockSpec((1, Q_BLOCK, KEY_WIN), lambda h: (h, 0, 0)),
        out_shape=jax.ShapeDtypeStruct((heads, Q_BLOCK, KEY_WIN), F32),
        compiler_params=_params(1),
        name="attn_bias_table",
    )(rb)


def _softmax_pv(s, v):
    m = jnp.max(s, axis=-1, keepdims=True)
    p = jnp.exp(s - m)
    l = jnp.sum(p, axis=-1, keepdims=True)
    return jnp.dot(p.astype(BF16), v, preferred_element_type=F32) / l


_NT = (((1,), (1,)), ((), ()))


def _attn_prompt_kernel(q_ref, k_ref, v_ref, zb_ref, t_ref, y_ref, nk_ref, nv_ref, kb_s, vb_s,
                        *, seq, keep):
    kb_s[...] = k_ref[0].astype(BF16)
    vb_s[...] = v_ref[0].astype(BF16)
    nk_ref[0] = k_ref[0, seq - keep:, :]
    nv_ref[0] = v_ref[0, seq - keep:, :]

    for qs in range(0, seq, Q_BLOCK):
        ks = max(0, qs - WINDOW)
        rows, keys = slice(qs, qs + Q_BLOCK), slice(ks, qs + Q_BLOCK)
        bias = t_ref[0, :, KEY_WIN - (keys.stop - keys.start):]
        s = lax.dot_general(q_ref[0, rows, :], kb_s[keys, :], _NT,
                            preferred_element_type=F32) + bias
        o = _softmax_pv(s, vb_s[keys, :])
        y_ref[0, rows, :] = (o * zb_ref[0, rows, :].astype(F32)).astype(BF16)


def _attn_prompt(q, kv, zb, table, batch, seq):
    db = q.shape[1]
    dh = db // N_HEADS
    keep = min(WINDOW, seq)
    assert seq % Q_BLOCK == 0 and dh % LANES == 0
    shape3 = (batch, seq, db)
    kv3 = kv.reshape(batch, seq, 2 * db)
    head = pl.BlockSpec((1, seq, dh), lambda b, h: (b, 0, h))
    v_head = pl.BlockSpec((1, seq, dh), lambda b, h: (b, 0, N_HEADS + h))
    kept = pl.BlockSpec((1, keep, dh), lambda b, h: (b, 0, h))
    return pl.pallas_call(
        functools.partial(_attn_prompt_kernel, seq=seq, keep=keep),
        grid=(batch, N_HEADS),
        in_specs=[head, head, v_head, head,
                  pl.BlockSpec((1, Q_BLOCK, KEY_WIN), lambda b, h: (h, 0, 0))],
        out_specs=[head, kept, kept],
        out_shape=[jax.ShapeDtypeStruct(shape3, BF16),
                   jax.ShapeDtypeStruct((batch, keep, db), F32),
                   jax.ShapeDtypeStruct((batch, keep, db), F32)],
        scratch_shapes=[pltpu.VMEM((seq, dh), BF16), pltpu.VMEM((seq, dh), BF16)],
        compiler_params=_params(2),
        name="attn_prompt",
    )(q.reshape(shape3), kv3, kv3, zb.reshape(shape3), table)


HEAD_GROUP = 8


def _attn_sample_kernel(q_ref, kn_ref, vn_ref, zb_ref, kc_ref, vc_ref, b_ref, y_ref):
    n_new, dh = q_ref.shape[1], q_ref.shape[-1]
    rows_cache = kc_ref.shape[2] * HEAD_GROUP
    rows_new = n_new * HEAD_GROUP
    for g in range(q_ref.shape[2] // HEAD_GROUP):
        heads = slice(g * HEAD_GROUP, (g + 1) * HEAD_GROUP)
        flat = lambda ref, *lead: ref[(*lead, slice(None), heads, slice(None))].reshape(-1, dh)
        q = flat(q_ref, 0).astype(BF16)
        s_c = lax.dot_general(q, flat(kc_ref, 0, 0).astype(BF16), _NT,
                              preferred_element_type=F32) + b_ref[g, :, :rows_cache]
        s_n = lax.dot_general(q, flat(kn_ref, 0).astype(BF16), _NT,
                              preferred_element_type=F32) + b_ref[g, :, rows_cache:rows_cache + rows_new]
        m = jnp.maximum(jnp.max(s_c, axis=-1, keepdims=True), jnp.max(s_n, axis=-1, keepdims=True))
        p_c = jnp.exp(s_c - m)
        p_n = jnp.exp(s_n - m)
        l = jnp.sum(p_c, axis=-1, keepdims=True) + jnp.sum(p_n, axis=-1, keepdims=True)
        o = (jnp.dot(p_c.astype(BF16), flat(vc_ref, 0, 0).astype(BF16), preferred_element_type=F32)
             + jnp.dot(p_n.astype(BF16), flat(vn_ref, 0).astype(BF16), preferred_element_type=F32))
        y_ref[0, :, heads, :] = (o / l * flat(zb_ref, 0)).reshape(n_new, HEAD_GROUP, dh)


def _sample_bias_kernel(a_ref, b_ref):
    rows = a_ref.shape[1]
    pos = lax.broadcasted_iota(jnp.int32, (LANES, LANES * HEAD_GROUP), 0)
    col = lax.broadcasted_iota(jnp.int32, (LANES, LANES * HEAD_GROUP), 1)
    expand = (col // HEAD_GROUP == pos).astype(F32)
    r = lax.broadcasted_iota(jnp.int32, (rows, LANES * HEAD_GROUP), 0)
    c = lax.broadcasted_iota(jnp.int32, (rows, LANES * HEAD_GROUP), 1)
    same_head = (r % HEAD_GROUP) == (c % HEAD_GROUP)
    for blk in range(a_ref.shape[2] // LANES):
        wide = jnp.dot(a_ref[0, :, blk * LANES:(blk + 1) * LANES], expand,
                       preferred_element_type=F32, precision=lax.Precision.HIGHEST)
        b_ref[0, :, blk * LANES * HEAD_GROUP:(blk + 1) * LANES * HEAD_GROUP] = jnp.where(
            same_head, wide, NEG_INF)


def _sample_bias(table, n_new, n_pos):
    groups = N_HEADS // HEAD_GROUP
    rows = n_new * HEAD_GROUP
    n_pad = -(-n_pos // LANES) * LANES
    a = table[:, :n_new, :n_pos].reshape(groups, HEAD_GROUP, n_new, n_pos).transpose(0, 2, 1, 3)
    a = jnp.pad(a.reshape(groups, rows, n_pos), ((0, 0), (0, 0), (0, n_pad - n_pos)))
    return pl.pallas_call(
        _sample_bias_kernel,
        grid=(groups,),
        in_specs=[pl.BlockSpec((1, rows, n_pad), lambda g: (g, 0, 0))],
        out_specs=pl.BlockSpec((1, rows, n_pad * HEAD_GROUP), lambda g: (g, 0, 0)),
        out_shape=jax.ShapeDtypeStruct((groups, rows, n_pad * HEAD_GROUP), F32),
        compiler_params=_params(1),
        name="attn_sample_bias",
    )(a)


def _attn_sample(q, kv, zb, k_cache, v_cache, layer, table, batch, n_new):
    db = q.shape[1]
    dh = db // N_HEADS
    n_cache = k_cache.shape[2]
    assert n_cache == WINDOW and n_new <= CHUNK and N_HEADS % HEAD_GROUP == 0
    bias = _sample_bias(table, n_new, n_cache + n_new)

    shape4 = (batch, n_new, N_HEADS, dh)
    new = pl.BlockSpec((1, n_new, N_HEADS, dh), lambda b: (b, 0, 0, 0))
    new_v = pl.BlockSpec((1, n_new, N_HEADS, dh), lambda b: (b, 0, 1, 0))
    cache = pl.BlockSpec((1, 1, n_cache, N_HEADS, dh), lambda b: (layer, b, 0, 0, 0))
    kv4 = kv.reshape(batch, n_new, 2 * N_HEADS, dh)
    y = pl.pallas_call(
        _attn_sample_kernel,
        grid=(batch,),
        in_specs=[new, new, new_v, new, cache, cache,
                  pl.BlockSpec(bias.shape, lambda b: (0, 0, 0))],
        out_specs=new,
        out_shape=jax.ShapeDtypeStruct(shape4, F32),
        compiler_params=_params(1),
        name="attn_sample",
    )(q.astype(F32).reshape(shape4), kv4, kv4, zb.astype(F32).reshape(shape4), k_cache, v_cache,
      bias)
    return y.reshape(batch * n_new, db).astype(BF16)


def _merge_kernel(ya_ref, yb_ref, wa_ref, wb_ref, ga_ref, gb_ref, m_ref):
    a = jnp.dot(ya_ref[...], wa_ref[...], preferred_element_type=F32)
    b = jnp.dot(yb_ref[...], wb_ref[...], preferred_element_type=F32)
    m_ref[...] = (ga_ref[...].astype(F32) * a + gb_ref[...].astype(F32) * b).astype(BF16)


def _merge(ya, yb, w_up, gates):
    m, kb = ya.shape
    n = w_up.shape[1] // 2
    tm, tn = _tile(m, 1024), _tile(n, 1024)
    nj = n // tn
    lhs = pl.BlockSpec((tm, kb), lambda i, j: (i, 0))
    rhs_a = pl.BlockSpec((kb, tn), lambda i, j: (0, j))
    rhs_b = pl.BlockSpec((kb, tn), lambda i, j: (0, nj + j))
    return pl.pallas_call(
        _merge_kernel,
        grid=(m // tm, nj),
        in_specs=[lhs, lhs, rhs_a, rhs_b,
                  pl.BlockSpec((tm, tn), lambda i, j: (i, j)),
                  pl.BlockSpec((tm, tn), lambda i, j: (i, nj + j))],
        out_specs=pl.BlockSpec((tm, tn), lambda i, j: (i, j)),
        out_shape=jax.ShapeDtypeStruct((m, n), BF16),
        compiler_params=_params(2),
        name="merge",
    )(ya, yb, w_up, w_up, gates, gates)


def _tail_kernel(m_ref, w_ref, p_ref, wp_ref, g_ref, x_hbm, y_ref, o_s, x1b_s, ssq_s, xbuf, sem,
                 *, nj, tm, tn):
    i = pl.program_id(0)
    j = pl.program_id(1)
    x_copy = pltpu.make_async_copy(x_hbm.at[pl.ds(i * tm, tm), :], xbuf, sem)

    @pl.when(j == 0)
    def _():
        x_copy.start()
        ssq_s[...] = jnp.zeros_like(ssq_s)

    @pl.when(j < nj)
    def _():
        t = jnp.dot(m_ref[...], w_ref[...], preferred_element_type=F32)
        o_s[j] = t
        ssq_s[...] += jnp.sum(t * t, axis=-1, keepdims=True)

    @pl.when(j == nj)
    def _():
        x_copy.wait()
        rstd = lax.rsqrt(ssq_s[...] / (nj * tn) + EPS)
        for jb in range(nj):
            cols = slice(jb * tn, (jb + 1) * tn)
            x1 = xbuf[:, cols] + o_s[jb] * rstd * g_ref[:, cols]
            o_s[jb] = x1
            x1b_s[:, cols] = x1.astype(BF16)

    @pl.when(j >= nj)
    def _():
        gate = jax.nn.sigmoid(jnp.dot(x1b_s[...], w_ref[...], preferred_element_type=F32))
        proj = jnp.dot(p_ref[...].astype(BF16), wp_ref[...], preferred_element_type=F32)
        y_ref[...] = o_s[j - nj] + gate * proj


def _tail(m_act, x, p, w_cat, w_pp, g):
    m, d = x.shape
    pd = p.shape[1]
    tm, tn = _tile(m, 512), _tile(d, 1024)
    nj = d // tn
    second = lambda i, j: (0, jnp.maximum(j - nj, 0))
    return pl.pallas_call(
        functools.partial(_tail_kernel, nj=nj, tm=tm, tn=tn),
        grid=(m // tm, 2 * nj),
        in_specs=[pl.BlockSpec((tm, d), lambda i, j: (i, 0)),
                  pl.BlockSpec((d, tn), lambda i, j: (0, j)),
                  pl.BlockSpec((tm, pd), lambda i, j: (i, 0)),
                  pl.BlockSpec((pd, tn), second),
                  pl.BlockSpec((1, d), lambda i, j: (0, 0)),
                  pl.BlockSpec(memory_space=pl.ANY)],
        out_specs=pl.BlockSpec((tm, tn), lambda i, j: (i, jnp.maximum(j - nj, 0))),
        out_shape=jax.ShapeDtypeStruct((m, d), F32),
        scratch_shapes=[pltpu.VMEM((nj, tm, tn), F32),
                        pltpu.VMEM((tm, d), BF16),
                        pltpu.VMEM((tm, 1), F32),
                        pltpu.VMEM((tm, d), F32),
                        pltpu.SemaphoreType.DMA(())],
        compiler_params=_params(2),
        name="tail",
    )(m_act, w_cat, p, w_pp, g.reshape(1, d), x)


def _layer_common(x, pre_g, wb, pending):
    d = x.shape[1]
    db = d // 2
    h = _rmsnorm_bf16(x, pre_g)
    scale = (db // N_HEADS) ** -0.5

    def run(wname, offs, n_cols, epilogue, out_dtype, name, then=(), tn_pref=1024):
        todo = [n for n in then if n in pending]
        outs = _project(h, wb[wname], offs, n_cols, epilogue, out_dtype, name, tn_pref,
                        side=[pending.pop(n) for n in todo])
        if not todo:
            return outs
        wb.update(zip(todo, outs[1:]))
        return outs[0]

    ug = run("gate", (0, db), db, _gmlp_gate, BF16, "proj_gmlp_gate", then=("v",), tn_pref=512)
    gv = run("v", (0,), db, _gelu, F32, "proj_gmlp_v", then=("q",))
    q = run("q", (0,), db, lambda a: a * scale, BF16, "proj_q", then=("kv", "zb"))
    kv = run("kv", (0,), 2 * db, lambda a: a, F32, "proj_kv", then=("gates",))
    zb = run("zb", (0,), db, jax.nn.silu, BF16, "proj_zb")
    gates = run("gates", (0,), 2 * d, jax.nn.sigmoid, BF16, "proj_merge_gates", then=("up", "tail"))
    return ug, gv, q, kv, zb, gates


def _layer_tail(x, p, ya, yb, gates, wb, post_g, w_pp):
    m = _merge(ya, yb, wb["up"], gates)
    return _tail(m, x, p, wb["tail"], w_pp, post_g)


def kernel(x_prompt, x_sample, cache_attn_k, cache_attn_v, p_prompt, p_sample, norm_pre_g, norm_post_g, w_in, gmlp_ln_g, gmlp_ln_b, gmlp_w_s, gmlp_b_s, attn_rel_bias, w_up_a, w_up_b, w_out, w_ple_gate, w_ple_proj):
    batch, seq, d = x_prompt.shape
    dec_batch, dec_seq, _ = x_sample.shape
    db = d // 2
    dh = db // N_HEADS
    depth = w_in.shape[0]
    assert seq % GMLP_CHUNK == 0 and dec_seq <= CHUNK

    xp = x_prompt.reshape(batch * seq, d)
    xs = x_sample.reshape(dec_batch * dec_seq, d)
    outs = [[] for _ in range(6)]
    for i in range(depth):
        wi = w_in[i]
        seg = lambda first, n=1: (wi, first * db, n * db)
        wb = {"gate": _convert_bf16([seg(0), seg(2)])}
        pending = {"v": [seg(1)], "q": [seg(3)], "kv": [seg(4, 2)], "zb": [seg(6)],
                   "gates": [seg(7), seg(8), seg(9), seg(10)],
                   "up": [(w_up_a[i], 0, d), (w_up_b[i], 0, d)],
                   "tail": [(w_out[i], 0, d), (w_ple_gate[i], 0, d)]}
        w_pp = w_ple_proj[i].astype(BF16)
        table = _bias_table(attn_rel_bias[i])
        sgu_w = (gmlp_w_s[i], gmlp_b_s[i], gmlp_ln_g[i], gmlp_ln_b[i])

        ug, gv, q, kv, zb, gates = _layer_common(xp, norm_pre_g[i], wb, pending)
        ya, gp = _sgu(gv, ug, *sgu_w, batch=batch, rows=GMLP_CHUNK, keep_last_only=True)
        yb, kp, vp = _attn_prompt(q, kv, zb, table, batch, seq)
        xp = _layer_tail(xp, p_prompt[i].reshape(batch * seq, -1), ya, yb.reshape(batch * seq, db),
                         gates, wb, norm_post_g[i], w_pp)

        ug, gv, q, kv, zb, gates = _layer_common(xs, norm_pre_g[i], wb, pending)
        ya, gs = _sgu(gv, ug, *sgu_w, batch=dec_batch, rows=dec_seq, keep_last_only=False)
        yb = _attn_sample(q, kv, zb, cache_attn_k, cache_attn_v, i, table, dec_batch, dec_seq)
        xs = _layer_tail(xs, p_sample[i].reshape(dec_batch * dec_seq, -1), ya, yb, gates, wb,
                         norm_post_g[i], w_pp)

        keep = kp.shape[1]
        for lst, val in zip(outs, (kp.reshape(batch, keep, N_HEADS, dh),
                                   vp.reshape(batch, keep, N_HEADS, dh),
                                   kv[:, :db].reshape(dec_batch, dec_seq, N_HEADS, dh),
                                   kv[:, db:].reshape(dec_batch, dec_seq, N_HEADS, dh),
                                   gp, gs)):
            lst.append(val)
    return (xp.reshape(batch, seq, d), xs.reshape(dec_batch, dec_seq, d),
            *[jnp.stack(lst) for lst in outs])
```

```python
import functools
import math

import jax
import jax.numpy as jnp
from jax import lax
from jax.experimental import pallas as pl
from jax.experimental.pallas import tpu as pltpu

CHUNK = 64
GMLP_CHUNK = 128
GMLP_GROUPS = 8
N_HEADS = 16
PAST_CHUNKS = 8
WINDOW = PAST_CHUNKS * CHUNK
REL_CLIP = 128
EPS = 1e-6
NEG_INF = -1e30

LANES = 128
Q_BLOCK = 256
KEY_WIN = WINDOW + Q_BLOCK
VMEM_LIMIT_BYTES = 56 * 1024 * 1024

F32 = jnp.float32
BF16 = jnp.bfloat16


def _params(n_axes):
    return pltpu.CompilerParams(dimension_semantics=("arbitrary",) * n_axes,
                                vmem_limit_bytes=VMEM_LIMIT_BYTES)


def _tile(n, pref):
    if n <= pref:
        return n
    t = (pref // LANES) * LANES
    while n % t:
        t -= LANES
    return t


def _rmsnorm_kernel(x_ref, g_ref, o_ref):
    x = x_ref[...]
    ms = jnp.mean(x * x, axis=-1, keepdims=True)
    o_ref[...] = (x * lax.rsqrt(ms + EPS) * g_ref[...]).astype(o_ref.dtype)


def _rmsnorm_bf16(x, g):
    m, d = x.shape
    tr = _tile(m, 256)
    return pl.pallas_call(
        _rmsnorm_kernel,
        grid=(m // tr,),
        in_specs=[pl.BlockSpec((tr, d), lambda i: (i, 0)),
                  pl.BlockSpec((1, d), lambda i: (0, 0))],
        out_specs=pl.BlockSpec((tr, d), lambda i: (i, 0)),
        out_shape=jax.ShapeDtypeStruct((m, d), BF16),
        compiler_params=_params(1),
        name="pre_rmsnorm",
    )(x, g.reshape(1, d))


def _proj_kernel(*refs, n_w, epilogue, side_layout):
    h = refs[0][...]
    accs = [jnp.dot(h, w[...], preferred_element_type=F32) for w in refs[1:1 + n_w]]
    n_side_in = sum(len(widths) for widths in side_layout)
    side_in = refs[1 + n_w:1 + n_w + n_side_in]
    o_ref = refs[1 + n_w + n_side_in]
    side_out = refs[2 + n_w + n_side_in:]
    o_ref[...] = epilogue(*accs).astype(o_ref.dtype)
    k = 0
    for out_ref, widths in zip(side_out, side_layout):
        off = 0
        for width in widths:
            out_ref[:, off:off + width] = side_in[k][...].astype(BF16)
            off += width
            k += 1


def _col_map(i, j, *, first):
    return (0, first + j)


def _slab_map(i, j, *, nj, col):
    return (i * nj + j, col)


def _project(h, w, col_offsets, n_cols, epilogue, out_dtype, name, tn_pref=1024, side=()):
    m, k = h.shape
    tm = _tile(m, 1024)
    tn = _tile(math.gcd(n_cols, *col_offsets), tn_pref)
    n_w = len(col_offsets)
    ni, nj = m // tm, n_cols // tn
    w_specs = [pl.BlockSpec((k, tn), functools.partial(_col_map, first=off // tn))
               for off in col_offsets]
    side_in, side_specs, side_out_specs, side_shapes, layout = [], [], [], [], []
    for group in side:
        rows = group[0][0].shape[0]
        slab = rows // (ni * nj)
        assert slab * ni * nj == rows and slab % 16 == 0
        total = sum(width for _, _, width in group)
        for src, col0, width in group:
            assert col0 % width == 0
            side_in.append(src)
            side_specs.append(pl.BlockSpec((slab, width),
                                           functools.partial(_slab_map, nj=nj, col=col0 // width)))
        side_out_specs.append(pl.BlockSpec((slab, total), functools.partial(_slab_map, nj=nj, col=0)))
        side_shapes.append(jax.ShapeDtypeStruct((rows, total), BF16))
        layout.append(tuple(width for _, _, width in group))
    outs = pl.pallas_call(
        functools.partial(_proj_kernel, n_w=n_w, epilogue=epilogue, side_layout=tuple(layout)),
        grid=(ni, nj),
        in_specs=[pl.BlockSpec((tm, k), lambda i, j: (i, 0))] + w_specs + side_specs,
        out_specs=[pl.BlockSpec((tm, tn), lambda i, j: (i, j))] + side_out_specs,
        out_shape=[jax.ShapeDtypeStruct((m, n_cols), out_dtype)] + side_shapes,
        compiler_params=_params(2),
        name=name,
    )(h, *([w] * n_w), *side_in)
    return outs[0] if not side else outs


def _convert_kernel(*refs):
    *pieces, out_ref = refs
    off = 0
    for piece in pieces:
        out_ref[:, off:off + piece.shape[1]] = piece[...].astype(BF16)
        off += piece.shape[1]


def _convert_bf16(group):
    rows = group[0][0].shape[0]
    slab = _tile(rows, 256)
    total = sum(width for _, _, width in group)
    return pl.pallas_call(
        _convert_kernel,
        grid=(rows // slab,),
        in_specs=[pl.BlockSpec((slab, width), functools.partial(lambda i, col: (i, col), col=col0 // width))
                  for _, col0, width in group],
        out_specs=pl.BlockSpec((slab, total), lambda i: (i, 0)),
        out_shape=jax.ShapeDtypeStruct((rows, total), BF16),
        compiler_params=_params(1),
        name="convert_weights",
    )(*[src for src, _, _ in group])


def _gelu(x):
    return 0.5 * x * (1.0 + lax.erf(x * math.sqrt(0.5)))


def _gmlp_gate(u, z):
    return _gelu(u) * jax.nn.silu(z)


SGU_CHUNKS_PER_STEP = 4


def _sgu_kernel(gv_ref, ug_ref, w_ref, bs_ref, lng_ref, lnb_ref, ya_ref, vn_ref, *, rows, gdim):
    kdim = w_ref.shape[2]
    i = lax.broadcasted_iota(jnp.int32, (rows, kdim), 0)
    j = lax.broadcasted_iota(jnp.int32, (rows, kdim), 1)
    visible = (j // CHUNK) <= (i // CHUNK)
    w = [jnp.where(visible, w_ref[g], 0.0).astype(BF16) for g in range(w_ref.shape[0])]
    for r0 in range(0, gv_ref.shape[0], rows):
        chunk = slice(r0, r0 + rows)
        x = gv_ref[chunk, :]
        xc = x - jnp.mean(x, axis=-1, keepdims=True)
        var = jnp.mean(xc * xc, axis=-1, keepdims=True)
        vn = xc * lax.rsqrt(var + EPS) * lng_ref[...] + lnb_ref[...]
        if r0 + rows == gv_ref.shape[0]:
            vn_ref[0] = vn
        vnb = vn.astype(BF16)
        if kdim > rows:
            vnb = jnp.concatenate([vnb, jnp.zeros((kdim - rows, vnb.shape[1]), BF16)], axis=0)
        for g, wg in enumerate(w):
            sl = slice(g * gdim, (g + 1) * gdim)
            s = jnp.dot(wg, vnb[:, sl], preferred_element_type=F32) + bs_ref[:, g:g + 1]
            ya_ref[chunk, sl] = (ug_ref[chunk, sl].astype(F32) * s).astype(BF16)


def _sgu(gv, ug, w_s, b_s, ln_g, ln_b, batch, rows, keep_last_only):
    m, db = gv.shape
    groups = w_s.shape[0]
    n_chunks = m // (batch * rows)
    per_step = math.gcd(n_chunks, SGU_CHUNKS_PER_STEP) if keep_last_only else 1
    n_steps = n_chunks // per_step
    kdim = max(rows, LANES)
    w = w_s[:, :rows, :rows]
    if kdim > rows:
        w = jnp.pad(w, ((0, 0), (0, 0), (0, kdim - rows)))
    bs_t = b_s[:, :rows].T
    row_map = lambda b, c: (b * n_steps + c, 0)
    full2 = lambda b, c: (0, 0)
    if keep_last_only:
        vn_shape, vn_spec = (batch, rows, db), pl.BlockSpec((1, rows, db), lambda b, c: (b, 0, 0))
    else:
        vn_shape = (batch * n_chunks, rows, db)
        vn_spec = pl.BlockSpec((1, rows, db), lambda b, c: (b * n_steps + c, 0, 0))
    step_rows = per_step * rows
    return pl.pallas_call(
        functools.partial(_sgu_kernel, rows=rows, gdim=db // groups),
        grid=(batch, n_steps),
        in_specs=[pl.BlockSpec((step_rows, db), row_map),
                  pl.BlockSpec((step_rows, db), row_map),
                  pl.BlockSpec((groups, rows, kdim), lambda b, c: (0, 0, 0)),
                  pl.BlockSpec((rows, groups), full2),
                  pl.BlockSpec((1, db), full2),
                  pl.BlockSpec((1, db), full2)],
        out_specs=[pl.BlockSpec((step_rows, db), row_map), vn_spec],
        out_shape=[jax.ShapeDtypeStruct((m, db), BF16), jax.ShapeDtypeStruct(vn_shape, F32)],
        compiler_params=_params(2),
        name="sgu",
    )(gv, ug, w, bs_t, ln_g.reshape(1, db), ln_b.reshape(1, db))


def _bias_kernel(rb_ref, t_ref):
    wide = KEY_WIN + Q_BLOCK
    n_tab = rb_ref.shape[1]
    pos = lax.broadcasted_iota(jnp.int32, (n_tab, wide), 1)
    tab = lax.broadcasted_iota(jnp.int32, (n_tab, wide), 0)
    idx = jnp.clip(KEY_WIN - pos, -REL_CLIP, REL_CLIP) + REL_CLIP
    base = jnp.sum(jnp.where(tab == idx, rb_ref[0], 0.0), axis=0, keepdims=True)
    rolled = pltpu.roll(jnp.broadcast_to(base, (Q_BLOCK, wide)), 0, 1, stride=1, stride_axis=0)
    t = rolled[:, Q_BLOCK:]
    qc = lax.broadcasted_iota(jnp.int32, (Q_BLOCK, KEY_WIN), 0) // CHUNK
    kc = lax.broadcasted_iota(jnp.int32, (Q_BLOCK, KEY_WIN), 1) // CHUNK
    t_ref[0] = jnp.where((qc <= kc) & (kc <= qc + PAST_CHUNKS), t, NEG_INF)


def _bias_table(rel_bias):
    heads, n_rel = rel_bias.shape
    n_tab = -(-n_rel // 8) * 8
    rb = jnp.pad(rel_bias, ((0, 0), (0, n_tab - n_rel))).reshape(heads, n_tab, 1)
    return pl.pallas_call(
        _bias_kernel,
        grid=(heads,),
        in_specs=[pl.BlockSpec((1, n_tab, 1), lambda h: (h, 0, 0))],
        out_specs=pl.BlockSpec((1, Q_BLOCK, KEY_WIN), lambda h: (h, 0, 0)),
        out_shape=jax.ShapeDtypeStruct((heads, Q_BLOCK, KEY_WIN), F32),
        compiler_params=_params(1),
        name="attn_bias_table",
    )(rb)


def _softmax_pv(s, v):
    m = jnp.max(s, axis=-1, keepdims=True)
    p = jnp.exp(s - m)
    l = jnp.sum(p, axis=-1, keepdims=True)
    return jnp.dot(p.astype(BF16), v, preferred_element_type=F32) / l


_NT = (((1,), (1,)), ((), ()))


def _attn_prompt_kernel(q_ref, k_ref, v_ref, zb_ref, t_ref, y_ref, nk_ref, nv_ref, kb_s, vb_s,
                        *, seq, keep):
    kb_s[...] = k_ref[0].astype(BF16)
    vb_s[...] = v_ref[0].astype(BF16)
    nk_ref[0] = k_ref[0, seq - keep:, :]
    nv_ref[0] = v_ref[0, seq - keep:, :]

    for qs in range(0, seq, Q_BLOCK):
        ks = max(0, qs - WINDOW)
        rows, keys = slice(qs, qs + Q_BLOCK), slice(ks, qs + Q_BLOCK)
        bias = t_ref[0, :, KEY_WIN - (keys.stop - keys.start):]
        s = lax.dot_general(q_ref[0, rows, :], kb_s[keys, :], _NT,
                            preferred_element_type=F32) + bias
        o = _softmax_pv(s, vb_s[keys, :])
        y_ref[0, rows, :] = (o * zb_ref[0, rows, :].astype(F32)).astype(BF16)


def _attn_prompt(q, kv, zb, table, batch, seq):
    db = q.shape[1]
    dh = db // N_HEADS
    keep = min(WINDOW, seq)
    assert seq % Q_BLOCK == 0 and dh % LANES == 0
    shape3 = (batch, seq, db)
    kv3 = kv.reshape(batch, seq, 2 * db)
    head = pl.BlockSpec((1, seq, dh), lambda b, h: (b, 0, h))
    v_head = pl.BlockSpec((1, seq, dh), lambda b, h: (b, 0, N_HEADS + h))
    kept = pl.BlockSpec((1, keep, dh), lambda b, h: (b, 0, h))
    return pl.pallas_call(
        functools.partial(_attn_prompt_kernel, seq=seq, keep=keep),
        grid=(batch, N_HEADS),
        in_specs=[head, head, v_head, head,
                  pl.BlockSpec((1, Q_BLOCK, KEY_WIN), lambda b, h: (h, 0, 0))],
        out_specs=[head, kept, kept],
        out_shape=[jax.ShapeDtypeStruct(shape3, BF16),
                   jax.ShapeDtypeStruct((batch, keep, db), F32),
                   jax.ShapeDtypeStruct((batch, keep, db), F32)],
        scratch_shapes=[pltpu.VMEM((seq, dh), BF16), pltpu.VMEM((seq, dh), BF16)],
        compiler_params=_params(2),
        name="attn_prompt",
    )(q.reshape(shape3), kv3, kv3, zb.reshape(shape3), table)


HEAD_GROUP = 8


def _attn_sample_kernel(q_ref, kn_ref, vn_ref, zb_ref, kc_ref, vc_ref, b_ref, y_ref):
    n_new, dh = q_ref.shape[1], q_ref.shape[-1]
    rows_cache = kc_ref.shape[2] * HEAD_GROUP
    rows_new = n_new * HEAD_GROUP
    for g in range(q_ref.shape[2] // HEAD_GROUP):
        heads = slice(g * HEAD_GROUP, (g + 1) * HEAD_GROUP)
        flat = lambda ref, *lead: ref[(*lead, slice(None), heads, slice(None))].reshape(-1, dh)
        q = flat(q_ref, 0).astype(BF16)
        s_c = lax.dot_general(q, flat(kc_ref, 0, 0).astype(BF16), _NT,
                              preferred_element_type=F32) + b_ref[g, :, :rows_cache]
        s_n = lax.dot_general(q, flat(kn_ref, 0).astype(BF16), _NT,
                              preferred_element_type=F32) + b_ref[g, :, rows_cache:rows_cache + rows_new]
        m = jnp.maximum(jnp.max(s_c, axis=-1, keepdims=True), jnp.max(s_n, axis=-1, keepdims=True))
        p_c = jnp.exp(s_c - m)
        p_n = jnp.exp(s_n - m)
        l = jnp.sum(p_c, axis=-1, keepdims=True) + jnp.sum(p_n, axis=-1, keepdims=True)
        o = (jnp.dot(p_c.astype(BF16), flat(vc_ref, 0, 0).astype(BF16), preferred_element_type=F32)
             + jnp.dot(p_n.astype(BF16), flat(vn_ref, 0).astype(BF16), preferred_element_type=F32))
        y_ref[0, :, heads, :] = (o / l * flat(zb_ref, 0)).reshape(n_new, HEAD_GROUP, dh)


def _sample_bias_kernel(a_ref, b_ref):
    rows = a_ref.shape[1]
    pos = lax.broadcasted_iota(jnp.int32, (LANES, LANES * HEAD_GROUP), 0)
    col = lax.broadcasted_iota(jnp.int32, (LANES, LANES * HEAD_GROUP), 1)
    expand = (col // HEAD_GROUP == pos).astype(F32)
    r = lax.broadcasted_iota(jnp.int32, (rows, LANES * HEAD_GROUP), 0)
    c = lax.broadcasted_iota(jnp.int32, (rows, LANES * HEAD_GROUP), 1)
    same_head = (r % HEAD_GROUP) == (c % HEAD_GROUP)
    for blk in range(a_ref.shape[2] // LANES):
        wide = jnp.dot(a_ref[0, :, blk * LANES:(blk + 1) * LANES], expand,
                       preferred_element_type=F32, precision=lax.Precision.HIGHEST)
        b_ref[0, :, blk * LANES * HEAD_GROUP:(blk + 1) * LANES * HEAD_GROUP] = jnp.where(
            same_head, wide, NEG_INF)


def _sample_bias(table, n_new, n_pos):
    groups = N_HEADS // HEAD_GROUP
    rows = n_new * HEAD_GROUP
    n_pad = -(-n_pos // LANES) * LANES
    a = table[:, :n_new, :n_pos].reshape(groups, HEAD_GROUP, n_new, n_pos).transpose(0, 2, 1, 3)
    a = jnp.pad(a.reshape(groups, rows, n_pos), ((0, 0), (0, 0), (0, n_pad - n_pos)))
    return pl.pallas_call(
        _sample_bias_kernel,
        grid=(groups,),
        in_specs=[pl.BlockSpec((1, rows, n_pad), lambda g: (g, 0, 0))],
        out_specs=pl.BlockSpec((1, rows, n_pad * HEAD_GROUP), lambda g: (g, 0, 0)),
        out_shape=jax.ShapeDtypeStruct((groups, rows, n_pad * HEAD_GROUP), F32),
        compiler_params=_params(1),
        name="attn_sample_bias",
    )(a)


def _attn_sample(q, kv, zb, k_cache, v_cache, layer, table, batch, n_new):
    db = q.shape[1]
    dh = db // N_HEADS
    n_cache = k_cache.shape[2]
    assert n_cache == WINDOW and n_new <= CHUNK and N_HEADS % HEAD_GROUP == 0
    bias = _sample_bias(table, n_new, n_cache + n_new)

    shape4 = (batch, n_new, N_HEADS, dh)
    new = pl.BlockSpec((1, n_new, N_HEADS, dh), lambda b: (b, 0, 0, 0))
    new_v = pl.BlockSpec((1, n_new, N_HEADS, dh), lambda b: (b, 0, 1, 0))
    cache = pl.BlockSpec((1, 1, n_cache, N_HEADS, dh), lambda b: (layer, b, 0, 0, 0))
    kv4 = kv.reshape(batch, n_new, 2 * N_HEADS, dh)
    y = pl.pallas_call(
        _attn_sample_kernel,
        grid=(batch,),
        in_specs=[new, new, new_v, new, cache, cache,
                  pl.BlockSpec(bias.shape, lambda b: (0, 0, 0))],
        out_specs=new,
        out_shape=jax.ShapeDtypeStruct(shape4, F32),
        compiler_params=_params(1),
        name="attn_sample",
    )(q.astype(F32).reshape(shape4), kv4, kv4, zb.astype(F32).reshape(shape4), k_cache, v_cache,
      bias)
    return y.reshape(batch * n_new, db).astype(BF16)


def _merge_kernel(ya_ref, yb_ref, wa_ref, wb_ref, ga_ref, gb_ref, m_ref):
    a = jnp.dot(ya_ref[...], wa_ref[...], preferred_element_type=F32)
    b = jnp.dot(yb_ref[...], wb_ref[...], preferred_element_type=F32)
    m_ref[...] = (ga_ref[...].astype(F32) * a + gb_ref[...].astype(F32) * b).astype(BF16)


def _merge(ya, yb, w_up, gates):
    m, kb = ya.shape
    n = w_up.shape[1] // 2
    tm, tn = _tile(m, 1024), _tile(n, 1024)
    nj = n // tn
    lhs = pl.BlockSpec((tm, kb), lambda i, j: (i, 0))
    rhs_a = pl.BlockSpec((kb, tn), lambda i, j: (0, j))
    rhs_b = pl.BlockSpec((kb, tn), lambda i, j: (0, nj + j))
    return pl.pallas_call(
        _merge_kernel,
        grid=(m // tm, nj),
        in_specs=[lhs, lhs, rhs_a, rhs_b,
                  pl.BlockSpec((tm, tn), lambda i, j: (i, j)),
                  pl.BlockSpec((tm, tn), lambda i, j: (i, nj + j))],
        out_specs=pl.BlockSpec((tm, tn), lambda i, j: (i, j)),
        out_shape=jax.ShapeDtypeStruct((m, n), BF16),
        compiler_params=_params(2),
        name="merge",
    )(ya, yb, w_up, w_up, gates, gates)


def _tail_kernel(m_ref, w_ref, p_ref, wp_ref, g_ref, x_hbm, y_ref, o_s, x1b_s, ssq_s, xbuf, sem,
                 *, nj, tm, tn):
    i = pl.program_id(0)
    j = pl.program_id(1)
    x_copy = pltpu.make_async_copy(x_hbm.at[pl.ds(i * tm, tm), :], xbuf, sem)

    @pl.when(j == 0)
    def _():
        x_copy.start()
        ssq_s[...] = jnp.zeros_like(ssq_s)

    @pl.when(j < nj)
    def _():
        t = jnp.dot(m_ref[...], w_ref[...], preferred_element_type=F32)
        o_s[j] = t
        ssq_s[...] += jnp.sum(t * t, axis=-1, keepdims=True)

    def finish(gate_logits, x1_tile):
        proj = jnp.dot(p_ref[...].astype(BF16), wp_ref[...], preferred_element_type=F32)
        y_ref[...] = x1_tile + jax.nn.sigmoid(gate_logits) * proj

    @pl.when(j == nj)
    def _():
        x_copy.wait()
        rstd = lax.rsqrt(ssq_s[...] / (nj * tn) + EPS)
        logits = None
        for jb in range(nj):
            cols = slice(jb * tn, (jb + 1) * tn)
            x1 = xbuf[:, cols] + o_s[jb] * rstd * g_ref[:, cols]
            o_s[jb] = x1
            x1b = x1.astype(BF16)
            x1b_s[:, cols] = x1b
            part = jnp.dot(x1b, w_ref[cols, :], preferred_element_type=F32)
            logits = part if logits is None else logits + part
        finish(logits, o_s[0])

    @pl.when(j > nj)
    def _():
        finish(jnp.dot(x1b_s[...], w_ref[...], preferred_element_type=F32), o_s[j - nj])


def _tail(m_act, x, p, w_cat, w_pp, g):
    m, d = x.shape
    pd = p.shape[1]
    tm, tn = _tile(m, 512), _tile(d, 1024)
    nj = d // tn
    second = lambda i, j: (0, jnp.maximum(j - nj, 0))
    return pl.pallas_call(
        functools.partial(_tail_kernel, nj=nj, tm=tm, tn=tn),
        grid=(m // tm, 2 * nj),
        in_specs=[pl.BlockSpec((tm, d), lambda i, j: (i, 0)),
                  pl.BlockSpec((d, tn), lambda i, j: (0, j)),
                  pl.BlockSpec((tm, pd), lambda i, j: (i, 0)),
                  pl.BlockSpec((pd, tn), second),
                  pl.BlockSpec((1, d), lambda i, j: (0, 0)),
                  pl.BlockSpec(memory_space=pl.ANY)],
        out_specs=pl.BlockSpec((tm, tn), lambda i, j: (i, jnp.maximum(j - nj, 0))),
        out_shape=jax.ShapeDtypeStruct((m, d), F32),
        scratch_shapes=[pltpu.VMEM((nj, tm, tn), F32),
                        pltpu.VMEM((tm, d), BF16),
                        pltpu.VMEM((tm, 1), F32),
                        pltpu.VMEM((tm, d), F32),
                        pltpu.SemaphoreType.DMA(())],
        compiler_params=_params(2),
        name="tail",
    )(m_act, w_cat, p, w_pp, g.reshape(1, d), x)


def _layer_common(x, pre_g, wb, pending):
    d = x.shape[1]
    db = d // 2
    h = _rmsnorm_bf16(x, pre_g)
    scale = (db // N_HEADS) ** -0.5

    def run(wname, offs, n_cols, epilogue, out_dtype, name, then=(), tn_pref=1024):
        todo = [n for n in then if n in pending]
        outs = _project(h, wb[wname], offs, n_cols, epilogue, out_dtype, name, tn_pref,
                        side=[pending.pop(n) for n in todo])
        if not todo:
            return outs
        wb.update(zip(todo, outs[1:]))
        return outs[0]

    ug = run("gate", (0, db), db, _gmlp_gate, BF16, "proj_gmlp_gate", then=("v",), tn_pref=512)
    gv = run("v", (0,), db, _gelu, F32, "proj_gmlp_v", then=("q",))
    q = run("q", (0,), db, lambda a: a * scale, BF16, "proj_q", then=("kv", "zb"))
    kv = run("kv", (0,), 2 * db, lambda a: a, F32, "proj_kv", then=("gates",))
    zb = run("zb", (0,), db, jax.nn.silu, BF16, "proj_zb")
    gates = run("gates", (0,), 2 * d, jax.nn.sigmoid, BF16, "proj_merge_gates", then=("up", "tail"))
    return ug, gv, q, kv, zb, gates


def _layer_tail(x, p, ya, yb, gates, wb, post_g, w_pp):
    m = _merge(ya, yb, wb["up"], gates)
    return _tail(m, x, p, wb["tail"], w_pp, post_g)


def kernel(x_prompt, x_sample, cache_attn_k, cache_attn_v, p_prompt, p_sample, norm_pre_g, norm_post_g, w_in, gmlp_ln_g, gmlp_ln_b, gmlp_w_s, gmlp_b_s, attn_rel_bias, w_up_a, w_up_b, w_out, w_ple_gate, w_ple_proj):
    batch, seq, d = x_prompt.shape
    dec_batch, dec_seq, _ = x_sample.shape
    db = d // 2
    dh = db // N_HEADS
    depth = w_in.shape[0]
    assert seq % GMLP_CHUNK == 0 and dec_seq <= CHUNK

    xp = x_prompt.reshape(batch * seq, d)
    xs = x_sample.reshape(dec_batch * dec_seq, d)
    outs = [[] for _ in range(6)]
    for i in range(depth):
        wi = w_in[i]
        seg = lambda first, n=1: (wi, first * db, n * db)
        wb = {"gate": _convert_bf16([seg(0), seg(2)])}
        pending = {"v": [seg(1)], "q": [seg(3)], "kv": [seg(4, 2)], "zb": [seg(6)],
                   "gates": [seg(7), seg(8), seg(9), seg(10)],
                   "up": [(w_up_a[i], 0, d), (w_up_b[i], 0, d)],
                   "tail": [(w_out[i], 0, d), (w_ple_gate[i], 0, d)]}
        w_pp = w_ple_proj[i].astype(BF16)
        table = _bias_table(attn_rel_bias[i])
        sgu_w = (gmlp_w_s[i], gmlp_b_s[i], gmlp_ln_g[i], gmlp_ln_b[i])

        ug, gv, q, kv, zb, gates = _layer_common(xp, norm_pre_g[i], wb, pending)
        ya, gp = _sgu(gv, ug, *sgu_w, batch=batch, rows=GMLP_CHUNK, keep_last_only=True)
        yb, kp, vp = _attn_prompt(q, kv, zb, table, batch, seq)
        xp = _layer_tail(xp, p_prompt[i].reshape(batch * seq, -1), ya, yb.reshape(batch * seq, db),
                         gates, wb, norm_post_g[i], w_pp)

        ug, gv, q, kv, zb, gates = _layer_common(xs, norm_pre_g[i], wb, pending)
        ya, gs = _sgu(gv, ug, *sgu_w, batch=dec_batch, rows=dec_seq, keep_last_only=False)
        yb = _attn_sample(q, kv, zb, cache_attn_k, cache_attn_v, i, table, dec_batch, dec_seq)
        xs = _layer_tail(xs, p_sample[i].reshape(dec_batch * dec_seq, -1), ya, yb, gates, wb,
                         norm_post_g[i], w_pp)

        keep = kp.shape[1]
        for lst, val in zip(outs, (kp.reshape(batch, keep, N_HEADS, dh),
                                   vp.reshape(batch, keep, N_HEADS, dh),
                                   kv[:, :db].reshape(dec_batch, dec_seq, N_HEADS, dh),
                                   kv[:, db:].reshape(dec_batch, dec_seq, N_HEADS, dh),
                                   gp, gs)):
            lst.append(val)
    return (xp.reshape(batch, seq, d), xs.reshape(dec_batch, dec_seq, d),
            *[jnp.stack(lst) for lst in outs])
```

```python
import functools
import math

import jax
import jax.numpy as jnp
from jax import lax
from jax.experimental import pallas as pl
from jax.experimental.pallas import tpu as pltpu

CHUNK = 64
GMLP_CHUNK = 128
GMLP_GROUPS = 8
N_HEADS = 16
PAST_CHUNKS = 8
WINDOW = PAST_CHUNKS * CHUNK
REL_CLIP = 128
EPS = 1e-6
NEG_INF = -1e30

LANES = 128
Q_BLOCK = 256
KEY_WIN = WINDOW + Q_BLOCK
VMEM_LIMIT_BYTES = 56 * 1024 * 1024

F32 = jnp.float32
BF16 = jnp.bfloat16


def _params(n_axes):
    return pltpu.CompilerParams(dimension_semantics=("arbitrary",) * n_axes,
                                vmem_limit_bytes=VMEM_LIMIT_BYTES)


def _tile(n, pref):
    if n <= pref:
        return n
    t = (pref // LANES) * LANES
    while n % t:
        t -= LANES
    return t


def _rmsnorm_kernel(x_ref, g_ref, o_ref):
    x = x_ref[...]
    ms = jnp.mean(x * x, axis=-1, keepdims=True)
    o_ref[...] = (x * lax.rsqrt(ms + EPS) * g_ref[...]).astype(o_ref.dtype)


def _rmsnorm_bf16(x, g):
    m, d = x.shape
    tr = _tile(m, 256)
    return pl.pallas_call(
        _rmsnorm_kernel,
        grid=(m // tr,),
        in_specs=[pl.BlockSpec((tr, d), lambda i: (i, 0)),
                  pl.BlockSpec((1, d), lambda i: (0, 0))],
        out_specs=pl.BlockSpec((tr, d), lambda i: (i, 0)),
        out_shape=jax.ShapeDtypeStruct((m, d), BF16),
        compiler_params=_params(1),
        name="pre_rmsnorm",
    )(x, g.reshape(1, d))


def _proj_kernel(*refs, n_w, epilogue, side_layout):
    h = refs[0][...]
    accs = [jnp.dot(h, w[...], preferred_element_type=F32) for w in refs[1:1 + n_w]]
    n_side_in = sum(len(widths) for widths in side_layout)
    side_in = refs[1 + n_w:1 + n_w + n_side_in]
    o_ref = refs[1 + n_w + n_side_in]
    side_out = refs[2 + n_w + n_side_in:]
    o_ref[...] = epilogue(*accs).astype(o_ref.dtype)
    k = 0
    for out_ref, widths in zip(side_out, side_layout):
        off = 0
        for width in widths:
            out_ref[:, off:off + width] = side_in[k][...].astype(BF16)
            off += width
            k += 1


def _col_map(i, j, *, first):
    return (0, first + j)


def _slab_map(i, j, *, nj, col):
    return (i * nj + j, col)


def _project(h, w, col_offsets, n_cols, epilogue, out_dtype, name, tn_pref=1024, side=()):
    m, k = h.shape
    tm = _tile(m, 1024)
    tn = _tile(math.gcd(n_cols, *col_offsets), tn_pref)
    n_w = len(col_offsets)
    ni, nj = m // tm, n_cols // tn
    w_specs = [pl.BlockSpec((k, tn), functools.partial(_col_map, first=off // tn))
               for off in col_offsets]
    side_in, side_specs, side_out_specs, side_shapes, layout = [], [], [], [], []
    for group in side:
        rows = group[0][0].shape[0]
        slab = rows // (ni * nj)
        assert slab * ni * nj == rows and slab % 16 == 0
        total = sum(width for _, _, width in group)
        for src, col0, width in group:
            assert col0 % width == 0
            side_in.append(src)
            side_specs.append(pl.BlockSpec((slab, width),
                                           functools.partial(_slab_map, nj=nj, col=col0 // width)))
        side_out_specs.append(pl.BlockSpec((slab, total), functools.partial(_slab_map, nj=nj, col=0)))
        side_shapes.append(jax.ShapeDtypeStruct((rows, total), BF16))
        layout.append(tuple(width for _, _, width in group))
    outs = pl.pallas_call(
        functools.partial(_proj_kernel, n_w=n_w, epilogue=epilogue, side_layout=tuple(layout)),
        grid=(ni, nj),
        in_specs=[pl.BlockSpec((tm, k), lambda i, j: (i, 0))] + w_specs + side_specs,
        out_specs=[pl.BlockSpec((tm, tn), lambda i, j: (i, j))] + side_out_specs,
        out_shape=[jax.ShapeDtypeStruct((m, n_cols), out_dtype)] + side_shapes,
        compiler_params=_params(2),
        name=name,
    )(h, *([w] * n_w), *side_in)
    return outs[0] if not side else outs


def _convert_kernel(*refs):
    *pieces, out_ref = refs
    off = 0
    for piece in pieces:
        out_ref[:, off:off + piece.shape[1]] = piece[...].astype(BF16)
        off += piece.shape[1]


def _convert_bf16(group):
    rows = group[0][0].shape[0]
    slab = _tile(rows, 256)
    total = sum(width for _, _, width in group)
    return pl.pallas_call(
        _convert_kernel,
        grid=(rows // slab,),
        in_specs=[pl.BlockSpec((slab, width), functools.partial(lambda i, col: (i, col), col=col0 // width))
                  for _, col0, width in group],
        out_specs=pl.BlockSpec((slab, total), lambda i: (i, 0)),
        out_shape=jax.ShapeDtypeStruct((rows, total), BF16),
        compiler_params=_params(1),
        name="convert_weights",
    )(*[src for src, _, _ in group])


def _gelu(x):
    return 0.5 * x * (1.0 + lax.erf(x * math.sqrt(0.5)))


def _gmlp_gate(u, z):
    return _gelu(u) * jax.nn.silu(z)


SGU_CHUNKS_PER_STEP = 4


def _sgu_kernel(gv_ref, ug_ref, w_ref, bs_ref, lng_ref, lnb_ref, ya_ref, vn_ref, *, rows, gdim):
    kdim = w_ref.shape[2]
    i = lax.broadcasted_iota(jnp.int32, (rows, kdim), 0)
    j = lax.broadcasted_iota(jnp.int32, (rows, kdim), 1)
    visible = (j // CHUNK) <= (i // CHUNK)
    w = [jnp.where(visible, w_ref[g], 0.0).astype(BF16) for g in range(w_ref.shape[0])]
    for r0 in range(0, gv_ref.shape[0], rows):
        chunk = slice(r0, r0 + rows)
        x = gv_ref[chunk, :]
        xc = x - jnp.mean(x, axis=-1, keepdims=True)
        var = jnp.mean(xc * xc, axis=-1, keepdims=True)
        vn = xc * lax.rsqrt(var + EPS) * lng_ref[...] + lnb_ref[...]
        if r0 + rows == gv_ref.shape[0]:
            vn_ref[0] = vn
        vnb = vn.astype(BF16)
        if kdim > rows:
            vnb = jnp.concatenate([vnb, jnp.zeros((kdim - rows, vnb.shape[1]), BF16)], axis=0)
        for g, wg in enumerate(w):
            sl = slice(g * gdim, (g + 1) * gdim)
            s = jnp.dot(wg, vnb[:, sl], preferred_element_type=F32) + bs_ref[:, g:g + 1]
            ya_ref[chunk, sl] = (ug_ref[chunk, sl].astype(F32) * s).astype(BF16)


def _sgu(gv, ug, w_s, b_s, ln_g, ln_b, batch, rows, keep_last_only):
    m, db = gv.shape
    groups = w_s.shape[0]
    n_chunks = m // (batch * rows)
    per_step = math.gcd(n_chunks, SGU_CHUNKS_PER_STEP) if keep_last_only else 1
    n_steps = n_chunks // per_step
    kdim = max(rows, LANES)
    w = w_s[:, :rows, :rows]
    if kdim > rows:
        w = jnp.pad(w, ((0, 0), (0, 0), (0, kdim - rows)))
    bs_t = b_s[:, :rows].T
    row_map = lambda b, c: (b * n_steps + c, 0)
    full2 = lambda b, c: (0, 0)
    if keep_last_only:
        vn_shape, vn_spec = (batch, rows, db), pl.BlockSpec((1, rows, db), lambda b, c: (b, 0, 0))
    else:
        vn_shape = (batch * n_chunks, rows, db)
        vn_spec = pl.BlockSpec((1, rows, db), lambda b, c: (b * n_steps + c, 0, 0))
    step_rows = per_step * rows
    return pl.pallas_call(
        functools.partial(_sgu_kernel, rows=rows, gdim=db // groups),
        grid=(batch, n_steps),
        in_specs=[pl.BlockSpec((step_rows, db), row_map),
                  pl.BlockSpec((step_rows, db), row_map),
                  pl.BlockSpec((groups, rows, kdim), lambda b, c: (0, 0, 0)),
                  pl.BlockSpec((rows, groups), full2),
                  pl.BlockSpec((1, db), full2),
                  pl.BlockSpec((1, db), full2)],
        out_specs=[pl.BlockSpec((step_rows, db), row_map), vn_spec],
        out_shape=[jax.ShapeDtypeStruct((m, db), BF16), jax.ShapeDtypeStruct(vn_shape, F32)],
        compiler_params=_params(2),
        name="sgu",
    )(gv, ug, w, bs_t, ln_g.reshape(1, db), ln_b.reshape(1, db))


def _bias_kernel(rb_ref, t_ref):
    wide = KEY_WIN + Q_BLOCK
    n_tab = rb_ref.shape[1]
    pos = lax.broadcasted_iota(jnp.int32, (n_tab, wide), 1)
    tab = lax.broadcasted_iota(jnp.int32, (n_tab, wide), 0)
    idx = jnp.clip(KEY_WIN - pos, -REL_CLIP, REL_CLIP) + REL_CLIP
    base = jnp.sum(jnp.where(tab == idx, rb_ref[0], 0.0), axis=0, keepdims=True)
    rolled = pltpu.roll(jnp.broadcast_to(base, (Q_BLOCK, wide)), 0, 1, stride=1, stride_axis=0)
    t = rolled[:, Q_BLOCK:]
    qc = lax.broadcasted_iota(jnp.int32, (Q_BLOCK, KEY_WIN), 0) // CHUNK
    kc = lax.broadcasted_iota(jnp.int32, (Q_BLOCK, KEY_WIN), 1) // CHUNK
    t_ref[0] = jnp.where((qc <= kc) & (kc <= qc + PAST_CHUNKS), t, NEG_INF)


def _bias_table(rel_bias):
    heads, n_rel = rel_bias.shape
    n_tab = -(-n_rel // 8) * 8
    rb = jnp.pad(rel_bias, ((0, 0), (0, n_tab - n_rel))).reshape(heads, n_tab, 1)
    return pl.pallas_call(
        _bias_kernel,
        grid=(heads,),
        in_specs=[pl.BlockSpec((1, n_tab, 1), lambda h: (h, 0, 0))],
        out_specs=pl.BlockSpec((1, Q_BLOCK, KEY_WIN), lambda h: (h, 0, 0)),
        out_shape=jax.ShapeDtypeStruct((heads, Q_BLOCK, KEY_WIN), F32),
        compiler_params=_params(1),
        name="attn_bias_table",
    )(rb)


def _softmax_pv(s, v):
    m = jnp.max(s, axis=-1, keepdims=True)
    p = jnp.exp(s - m)
    l = jnp.sum(p, axis=-1, keepdims=True)
    return jnp.dot(p.astype(BF16), v, preferred_element_type=F32) / l


_NT = (((1,), (1,)), ((), ()))


def _attn_prompt_kernel(q_ref, k_ref, v_ref, zb_ref, t_ref, y_ref, nk_ref, nv_ref, kb_s, vb_s,
                        *, seq, keep, dh):
    kb_s[...] = k_ref[0].astype(BF16)
    vb_s[...] = v_ref[0].astype(BF16)
    nk_ref[0] = k_ref[0, seq - keep:, :]
    nv_ref[0] = v_ref[0, seq - keep:, :]

    for h in range(q_ref.shape[2] // dh):
        cols = slice(h * dh, (h + 1) * dh)
        for qs in range(0, seq, Q_BLOCK):
            ks = max(0, qs - WINDOW)
            rows, keys = slice(qs, qs + Q_BLOCK), slice(ks, qs + Q_BLOCK)
            bias = t_ref[h, :, KEY_WIN - (keys.stop - keys.start):]
            s = lax.dot_general(q_ref[0, rows, cols], kb_s[keys, cols], _NT,
                                preferred_element_type=F32) + bias
            o = _softmax_pv(s, vb_s[keys, cols])
            y_ref[0, rows, cols] = (o * zb_ref[0, rows, cols].astype(F32)).astype(BF16)


ATTN_HEADS_PER_STEP = 2


def _attn_prompt(q, kv, zb, table, batch, seq):
    db = q.shape[1]
    dh = db // N_HEADS
    keep = min(WINDOW, seq)
    assert seq % Q_BLOCK == 0 and dh % LANES == 0
    shape3 = (batch, seq, db)
    kv3 = kv.reshape(batch, seq, 2 * db)
    hps = math.gcd(N_HEADS, ATTN_HEADS_PER_STEP)
    n_steps = N_HEADS // hps
    head = pl.BlockSpec((1, seq, hps * dh), lambda b, h: (b, 0, h))
    v_head = pl.BlockSpec((1, seq, hps * dh), lambda b, h: (b, 0, n_steps + h))
    kept = pl.BlockSpec((1, keep, hps * dh), lambda b, h: (b, 0, h))
    return pl.pallas_call(
        functools.partial(_attn_prompt_kernel, seq=seq, keep=keep, dh=dh),
        grid=(batch, n_steps),
        in_specs=[head, head, v_head, head,
                  pl.BlockSpec((hps, Q_BLOCK, KEY_WIN), lambda b, h: (h, 0, 0))],
        out_specs=[head, kept, kept],
        out_shape=[jax.ShapeDtypeStruct(shape3, BF16),
                   jax.ShapeDtypeStruct((batch, keep, db), F32),
                   jax.ShapeDtypeStruct((batch, keep, db), F32)],
        scratch_shapes=[pltpu.VMEM((seq, hps * dh), BF16), pltpu.VMEM((seq, hps * dh), BF16)],
        compiler_params=_params(2),
        name="attn_prompt",
    )(q.reshape(shape3), kv3, kv3, zb.reshape(shape3), table)


HEAD_GROUP = 8


def _attn_sample_kernel(q_ref, kn_ref, vn_ref, zb_ref, kc_ref, vc_ref, b_ref, y_ref):
    n_new, dh = q_ref.shape[1], q_ref.shape[-1]
    rows_cache = kc_ref.shape[2] * HEAD_GROUP
    rows_new = n_new * HEAD_GROUP
    for g in range(q_ref.shape[2] // HEAD_GROUP):
        heads = slice(g * HEAD_GROUP, (g + 1) * HEAD_GROUP)
        flat = lambda ref, *lead: ref[(*lead, slice(None), heads, slice(None))].reshape(-1, dh)
        q = flat(q_ref, 0).astype(BF16)
        s_c = lax.dot_general(q, flat(kc_ref, 0, 0).astype(BF16), _NT,
                              preferred_element_type=F32) + b_ref[g, :, :rows_cache]
        s_n = lax.dot_general(q, flat(kn_ref, 0).astype(BF16), _NT,
                              preferred_element_type=F32) + b_ref[g, :, rows_cache:rows_cache + rows_new]
        m = jnp.maximum(jnp.max(s_c, axis=-1, keepdims=True), jnp.max(s_n, axis=-1, keepdims=True))
        p_c = jnp.exp(s_c - m)
        p_n = jnp.exp(s_n - m)
        l = jnp.sum(p_c, axis=-1, keepdims=True) + jnp.sum(p_n, axis=-1, keepdims=True)
        o = (jnp.dot(p_c.astype(BF16), flat(vc_ref, 0, 0).astype(BF16), preferred_element_type=F32)
             + jnp.dot(p_n.astype(BF16), flat(vn_ref, 0).astype(BF16), preferred_element_type=F32))
        y_ref[0, :, heads, :] = (o / l * flat(zb_ref, 0)).reshape(n_new, HEAD_GROUP, dh)


def _sample_bias_kernel(a_ref, b_ref):
    rows = a_ref.shape[1]
    pos = lax.broadcasted_iota(jnp.int32, (LANES, LANES * HEAD_GROUP), 0)
    col = lax.broadcasted_iota(jnp.int32, (LANES, LANES * HEAD_GROUP), 1)
    expand = (col // HEAD_GROUP == pos).astype(F32)
    r = lax.broadcasted_iota(jnp.int32, (rows, LANES * HEAD_GROUP), 0)
    c = lax.broadcasted_iota(jnp.int32, (rows, LANES * HEAD_GROUP), 1)
    same_head = (r % HEAD_GROUP) == (c % HEAD_GROUP)
    for blk in range(a_ref.shape[2] // LANES):
        wide = jnp.dot(a_ref[0, :, blk * LANES:(blk + 1) * LANES], expand,
                       preferred_element_type=F32, precision=lax.Precision.HIGHEST)
        b_ref[0, :, blk * LANES * HEAD_GROUP:(blk + 1) * LANES * HEAD_GROUP] = jnp.where(
            same_head, wide, NEG_INF)


def _sample_bias(table, n_new, n_pos):
    groups = N_HEADS // HEAD_GROUP
    rows = n_new * HEAD_GROUP
    n_pad = -(-n_pos // LANES) * LANES
    a = table[:, :n_new, :n_pos].reshape(groups, HEAD_GROUP, n_new, n_pos).transpose(0, 2, 1, 3)
    a = jnp.pad(a.reshape(groups, rows, n_pos), ((0, 0), (0, 0), (0, n_pad - n_pos)))
    return pl.pallas_call(
        _sample_bias_kernel,
        grid=(groups,),
        in_specs=[pl.BlockSpec((1, rows, n_pad), lambda g: (g, 0, 0))],
        out_specs=pl.BlockSpec((1, rows, n_pad * HEAD_GROUP), lambda g: (g, 0, 0)),
        out_shape=jax.ShapeDtypeStruct((groups, rows, n_pad * HEAD_GROUP), F32),
        compiler_params=_params(1),
        name="attn_sample_bias",
    )(a)


def _attn_sample(q, kv, zb, k_cache, v_cache, layer, table, batch, n_new):
    db = q.shape[1]
    dh = db // N_HEADS
    n_cache = k_cache.shape[2]
    assert n_cache == WINDOW and n_new <= CHUNK and N_HEADS % HEAD_GROUP == 0
    bias = _sample_bias(table, n_new, n_cache + n_new)

    shape4 = (batch, n_new, N_HEADS, dh)
    new = pl.BlockSpec((1, n_new, N_HEADS, dh), lambda b: (b, 0, 0, 0))
    new_v = pl.BlockSpec((1, n_new, N_HEADS, dh), lambda b: (b, 0, 1, 0))
    cache = pl.BlockSpec((1, 1, n_cache, N_HEADS, dh), lambda b: (layer, b, 0, 0, 0))
    kv4 = kv.reshape(batch, n_new, 2 * N_HEADS, dh)
    y = pl.pallas_call(
        _attn_sample_kernel,
        grid=(batch,),
        in_specs=[new, new, new_v, new, cache, cache,
                  pl.BlockSpec(bias.shape, lambda b: (0, 0, 0))],
        out_specs=new,
        out_shape=jax.ShapeDtypeStruct(shape4, F32),
        compiler_params=_params(1),
        name="attn_sample",
    )(q.astype(F32).reshape(shape4), kv4, kv4, zb.astype(F32).reshape(shape4), k_cache, v_cache,
      bias)
    return y.reshape(batch * n_new, db).astype(BF16)


def _merge_kernel(ya_ref, yb_ref, wa_ref, wb_ref, ga_ref, gb_ref, m_ref):
    a = jnp.dot(ya_ref[...], wa_ref[...], preferred_element_type=F32)
    b = jnp.dot(yb_ref[...], wb_ref[...], preferred_element_type=F32)
    m_ref[...] = (ga_ref[...].astype(F32) * a + gb_ref[...].astype(F32) * b).astype(BF16)


def _merge(ya, yb, w_up, gates):
    m, kb = ya.shape
    n = w_up.shape[1] // 2
    tm, tn = _tile(m, 1024), _tile(n, 1024)
    nj = n // tn
    lhs = pl.BlockSpec((tm, kb), lambda i, j: (i, 0))
    rhs_a = pl.BlockSpec((kb, tn), lambda i, j: (0, j))
    rhs_b = pl.BlockSpec((kb, tn), lambda i, j: (0, nj + j))
    return pl.pallas_call(
        _merge_kernel,
        grid=(m // tm, nj),
        in_specs=[lhs, lhs, rhs_a, rhs_b,
                  pl.BlockSpec((tm, tn), lambda i, j: (i, j)),
                  pl.BlockSpec((tm, tn), lambda i, j: (i, nj + j))],
        out_specs=pl.BlockSpec((tm, tn), lambda i, j: (i, j)),
        out_shape=jax.ShapeDtypeStruct((m, n), BF16),
        compiler_params=_params(2),
        name="merge",
    )(ya, yb, w_up, w_up, gates, gates)


def _tail_kernel(m_ref, w_ref, p_ref, wp_ref, g_ref, x_hbm, y_ref, o_s, x1b_s, ssq_s, xbuf, sem,
                 *, nj, tm, tn):
    i = pl.program_id(0)
    j = pl.program_id(1)
    x_copy = pltpu.make_async_copy(x_hbm.at[pl.ds(i * tm, tm), :], xbuf, sem)

    @pl.when(j == 0)
    def _():
        x_copy.start()
        ssq_s[...] = jnp.zeros_like(ssq_s)

    @pl.when(j < nj)
    def _():
        t = jnp.dot(m_ref[...], w_ref[...], preferred_element_type=F32)
        o_s[j] = t
        ssq_s[...] += jnp.sum(t * t, axis=-1, keepdims=True)

    def finish(gate_logits, x1_tile):
        proj = jnp.dot(p_ref[...].astype(BF16), wp_ref[...], preferred_element_type=F32)
        y_ref[...] = x1_tile + jax.nn.sigmoid(gate_logits) * proj

    @pl.when(j == nj)
    def _():
        x_copy.wait()
        rstd = lax.rsqrt(ssq_s[...] / (nj * tn) + EPS)
        logits = None
        for jb in range(nj):
            cols = slice(jb * tn, (jb + 1) * tn)
            x1 = xbuf[:, cols] + o_s[jb] * rstd * g_ref[:, cols]
            o_s[jb] = x1
            x1b = x1.astype(BF16)
            x1b_s[:, cols] = x1b
            part = jnp.dot(x1b, w_ref[cols, :], preferred_element_type=F32)
            logits = part if logits is None else logits + part
        finish(logits, o_s[0])

    @pl.when(j > nj)
    def _():
        finish(jnp.dot(x1b_s[...], w_ref[...], preferred_element_type=F32), o_s[j - nj])


def _tail(m_act, x, p, w_cat, w_pp, g):
    m, d = x.shape
    pd = p.shape[1]
    tm, tn = _tile(m, 512), _tile(d, 1024)
    nj = d // tn
    second = lambda i, j: (0, jnp.maximum(j - nj, 0))
    return pl.pallas_call(
        functools.partial(_tail_kernel, nj=nj, tm=tm, tn=tn),
        grid=(m // tm, 2 * nj),
        in_specs=[pl.BlockSpec((tm, d), lambda i, j: (i, 0)),
                  pl.BlockSpec((d, tn), lambda i, j: (0, j)),
                  pl.BlockSpec((tm, pd), lambda i, j: (i, 0)),
                  pl.BlockSpec((pd, tn), second),
                  pl.BlockSpec((1, d), lambda i, j: (0, 0)),
                  pl.BlockSpec(memory_space=pl.ANY)],
        out_specs=pl.BlockSpec((tm, tn), lambda i, j: (i, jnp.maximum(j - nj, 0))),
        out_shape=jax.ShapeDtypeStruct((m, d), F32),
        scratch_shapes=[pltpu.VMEM((nj, tm, tn), F32),
                        pltpu.VMEM((tm, d), BF16),
                        pltpu.VMEM((tm, 1), F32),
                        pltpu.VMEM((tm, d), F32),
                        pltpu.SemaphoreType.DMA(())],
        compiler_params=_params(2),
        name="tail",
    )(m_act, w_cat, p, w_pp, g.reshape(1, d), x)


def _layer_common(x, pre_g, wb, pending):
    d = x.shape[1]
    db = d // 2
    h = _rmsnorm_bf16(x, pre_g)
    scale = (db // N_HEADS) ** -0.5

    def run(wname, offs, n_cols, epilogue, out_dtype, name, then=(), tn_pref=1024):
        todo = [n for n in then if n in pending]
        outs = _project(h, wb[wname], offs, n_cols, epilogue, out_dtype, name, tn_pref,
                        side=[pending.pop(n) for n in todo])
        if not todo:
            return outs
        wb.update(zip(todo, outs[1:]))
        return outs[0]

    ug = run("gate", (0, db), db, _gmlp_gate, BF16, "proj_gmlp_gate", then=("v",), tn_pref=512)
    gv = run("v", (0,), db, _gelu, F32, "proj_gmlp_v", then=("q",))
    q = run("q", (0,), db, lambda a: a * scale, BF16, "proj_q", then=("kv", "zb"))
    kv = run("kv", (0,), 2 * db, lambda a: a, F32, "proj_kv", then=("gates",))
    zb = run("zb", (0,), db, jax.nn.silu, BF16, "proj_zb")
    gates = run("gates", (0,), 2 * d, jax.nn.sigmoid, BF16, "proj_merge_gates", then=("up", "tail"))
    return ug, gv, q, kv, zb, gates


def _layer_tail(x, p, ya, yb, gates, wb, post_g, w_pp):
    m = _merge(ya, yb, wb["up"], gates)
    return _tail(m, x, p, wb["tail"], w_pp, post_g)


def kernel(x_prompt, x_sample, cache_attn_k, cache_attn_v, p_prompt, p_sample, norm_pre_g, norm_post_g, w_in, gmlp_ln_g, gmlp_ln_b, gmlp_w_s, gmlp_b_s, attn_rel_bias, w_up_a, w_up_b, w_out, w_ple_gate, w_ple_proj):
    batch, seq, d = x_prompt.shape
    dec_batch, dec_seq, _ = x_sample.shape
    db = d // 2
    dh = db // N_HEADS
    depth = w_in.shape[0]
    assert seq % GMLP_CHUNK == 0 and dec_seq <= CHUNK

    xp = x_prompt.reshape(batch * seq, d)
    xs = x_sample.reshape(dec_batch * dec_seq, d)
    outs = [[] for _ in range(6)]
    for i in range(depth):
        wi = w_in[i]
        seg = lambda first, n=1: (wi, first * db, n * db)
        wb = {"gate": _convert_bf16([seg(0), seg(2)])}
        pending = {"v": [seg(1)], "q": [seg(3)], "kv": [seg(4, 2)], "zb": [seg(6)],
                   "gates": [seg(7), seg(8), seg(9), seg(10)],
                   "up": [(w_up_a[i], 0, d), (w_up_b[i], 0, d)],
                   "tail": [(w_out[i], 0, d), (w_ple_gate[i], 0, d)]}
        w_pp = w_ple_proj[i].astype(BF16)
        table = _bias_table(attn_rel_bias[i])
        sgu_w = (gmlp_w_s[i], gmlp_b_s[i], gmlp_ln_g[i], gmlp_ln_b[i])

        ug, gv, q, kv, zb, gates = _layer_common(xp, norm_pre_g[i], wb, pending)
        ya, gp = _sgu(gv, ug, *sgu_w, batch=batch, rows=GMLP_CHUNK, keep_last_only=True)
        yb, kp, vp = _attn_prompt(q, kv, zb, table, batch, seq)
        xp = _layer_tail(xp, p_prompt[i].reshape(batch * seq, -1), ya, yb.reshape(batch * seq, db),
                         gates, wb, norm_post_g[i], w_pp)

        ug, gv, q, kv, zb, gates = _layer_common(xs, norm_pre_g[i], wb, pending)
        ya, gs = _sgu(gv, ug, *sgu_w, batch=dec_batch, rows=dec_seq, keep_last_only=False)
        yb = _attn_sample(q, kv, zb, cache_attn_k, cache_attn_v, i, table, dec_batch, dec_seq)
        xs = _layer_tail(xs, p_sample[i].reshape(dec_batch * dec_seq, -1), ya, yb, gates, wb,
                         norm_post_g[i], w_pp)

        keep = kp.shape[1]
        for lst, val in zip(outs, (kp.reshape(batch, keep, N_HEADS, dh),
                                   vp.reshape(batch, keep, N_HEADS, dh),
                                   kv[:, :db].reshape(dec_batch, dec_seq, N_HEADS, dh),
                                   kv[:, db:].reshape(dec_batch, dec_seq, N_HEADS, dh),
                                   gp, gs)):
            lst.append(val)
    return (xp.reshape(batch, seq, d), xs.reshape(dec_batch, dec_seq, d),
            *[jnp.stack(lst) for lst in outs])
```

```python
import functools
import math

import jax
import jax.numpy as jnp
from jax import lax
from jax.experimental import pallas as pl
from jax.experimental.pallas import tpu as pltpu

CHUNK = 64
GMLP_CHUNK = 128
GMLP_GROUPS = 8
N_HEADS = 16
PAST_CHUNKS = 8
WINDOW = PAST_CHUNKS * CHUNK
REL_CLIP = 128
EPS = 1e-6
NEG_INF = -1e30

LANES = 128
Q_BLOCK = 256
KEY_WIN = WINDOW + Q_BLOCK
VMEM_LIMIT_BYTES = 56 * 1024 * 1024
MATMUL_TILE = 1024
TWO_WEIGHT_COLS = 512
TAIL_ROWS = 512
ROWWISE_ROWS = 256

F32 = jnp.float32
BF16 = jnp.bfloat16


def _params(n_axes):
    return pltpu.CompilerParams(dimension_semantics=("arbitrary",) * n_axes,
                                vmem_limit_bytes=VMEM_LIMIT_BYTES)


def _tile(n, pref):
    if n <= pref:
        return n
    t = (pref // LANES) * LANES
    while n % t:
        t -= LANES
    return t


def _rmsnorm_kernel(x_ref, g_ref, o_ref):
    x = x_ref[...]
    ms = jnp.mean(x * x, axis=-1, keepdims=True)
    o_ref[...] = (x * lax.rsqrt(ms + EPS) * g_ref[...]).astype(o_ref.dtype)


def _rmsnorm_bf16(x, g):
    m, d = x.shape
    tr = _tile(m, ROWWISE_ROWS)
    return pl.pallas_call(
        _rmsnorm_kernel,
        grid=(m // tr,),
        in_specs=[pl.BlockSpec((tr, d), lambda i: (i, 0)),
                  pl.BlockSpec((1, d), lambda i: (0, 0))],
        out_specs=pl.BlockSpec((tr, d), lambda i: (i, 0)),
        out_shape=jax.ShapeDtypeStruct((m, d), BF16),
        compiler_params=_params(1),
        name="pre_rmsnorm",
    )(x, g.reshape(1, d))


def _proj_kernel(*refs, n_w, epilogue, side_layout):
    h = refs[0][...]
    accs = [jnp.dot(h, w[...], preferred_element_type=F32) for w in refs[1:1 + n_w]]
    n_side_in = sum(len(widths) for widths in side_layout)
    side_in = refs[1 + n_w:1 + n_w + n_side_in]
    o_ref = refs[1 + n_w + n_side_in]
    side_out = refs[2 + n_w + n_side_in:]
    o_ref[...] = epilogue(*accs).astype(o_ref.dtype)
    k = 0
    for out_ref, widths in zip(side_out, side_layout):
        off = 0
        for width in widths:
            out_ref[:, off:off + width] = side_in[k][...].astype(BF16)
            off += width
            k += 1


def _col_map(i, j, *, first):
    return (0, first + j)


def _slab_map(i, j, *, nj, col):
    return (i * nj + j, col)


def _project(h, w, col_offsets, n_cols, epilogue, out_dtype, name, tn_pref=MATMUL_TILE, side=()):
    m, k = h.shape
    tm = _tile(m, MATMUL_TILE)
    tn = _tile(math.gcd(n_cols, *col_offsets), tn_pref)
    n_w = len(col_offsets)
    ni, nj = m // tm, n_cols // tn
    w_specs = [pl.BlockSpec((k, tn), functools.partial(_col_map, first=off // tn))
               for off in col_offsets]
    side_in, side_specs, side_out_specs, side_shapes, layout = [], [], [], [], []
    for group in side:
        rows = group[0][0].shape[0]
        slab = rows // (ni * nj)
        assert slab * ni * nj == rows and slab % 16 == 0
        total = sum(width for _, _, width in group)
        for src, col0, width in group:
            assert col0 % width == 0
            side_in.append(src)
            side_specs.append(pl.BlockSpec((slab, width),
                                           functools.partial(_slab_map, nj=nj, col=col0 // width)))
        side_out_specs.append(pl.BlockSpec((slab, total), functools.partial(_slab_map, nj=nj, col=0)))
        side_shapes.append(jax.ShapeDtypeStruct((rows, total), BF16))
        layout.append(tuple(width for _, _, width in group))
    outs = pl.pallas_call(
        functools.partial(_proj_kernel, n_w=n_w, epilogue=epilogue, side_layout=tuple(layout)),
        grid=(ni, nj),
        in_specs=[pl.BlockSpec((tm, k), lambda i, j: (i, 0))] + w_specs + side_specs,
        out_specs=[pl.BlockSpec((tm, tn), lambda i, j: (i, j))] + side_out_specs,
        out_shape=[jax.ShapeDtypeStruct((m, n_cols), out_dtype)] + side_shapes,
        compiler_params=_params(2),
        name=name,
    )(h, *([w] * n_w), *side_in)
    return outs[0] if not side else outs


def _convert_kernel(*refs):
    *pieces, out_ref = refs
    off = 0
    for piece in pieces:
        out_ref[:, off:off + piece.shape[1]] = piece[...].astype(BF16)
        off += piece.shape[1]


def _convert_bf16(group):
    rows = group[0][0].shape[0]
    slab = _tile(rows, ROWWISE_ROWS)
    total = sum(width for _, _, width in group)
    return pl.pallas_call(
        _convert_kernel,
        grid=(rows // slab,),
        in_specs=[pl.BlockSpec((slab, width), functools.partial(lambda i, col: (i, col), col=col0 // width))
                  for _, col0, width in group],
        out_specs=pl.BlockSpec((slab, total), lambda i: (i, 0)),
        out_shape=jax.ShapeDtypeStruct((rows, total), BF16),
        compiler_params=_params(1),
        name="convert_weights",
    )(*[src for src, _, _ in group])


def _gelu(x):
    return 0.5 * x * (1.0 + lax.erf(x * math.sqrt(0.5)))


def _gmlp_gate(u, z):
    return _gelu(u) * jax.nn.silu(z)


SGU_CHUNKS_PER_STEP = 8


def _sgu_kernel(gv_ref, ug_ref, w_ref, bs_ref, lng_ref, lnb_ref, ya_ref, vn_ref, *, rows, gdim):
    kdim = w_ref.shape[2]
    i = lax.broadcasted_iota(jnp.int32, (rows, kdim), 0)
    j = lax.broadcasted_iota(jnp.int32, (rows, kdim), 1)
    visible = (j // CHUNK) <= (i // CHUNK)
    w = [jnp.where(visible, w_ref[g], 0.0).astype(BF16) for g in range(w_ref.shape[0])]
    for r0 in range(0, gv_ref.shape[0], rows):
        chunk = slice(r0, r0 + rows)
        x = gv_ref[chunk, :]
        xc = x - jnp.mean(x, axis=-1, keepdims=True)
        var = jnp.mean(xc * xc, axis=-1, keepdims=True)
        vn = xc * lax.rsqrt(var + EPS) * lng_ref[...] + lnb_ref[...]
        if r0 + rows == gv_ref.shape[0]:
            vn_ref[0] = vn
        vnb = vn.astype(BF16)
        if kdim > rows:
            vnb = jnp.concatenate([vnb, jnp.zeros((kdim - rows, vnb.shape[1]), BF16)], axis=0)
        for g, wg in enumerate(w):
            sl = slice(g * gdim, (g + 1) * gdim)
            s = jnp.dot(wg, vnb[:, sl], preferred_element_type=F32) + bs_ref[:, g:g + 1]
            ya_ref[chunk, sl] = (ug_ref[chunk, sl].astype(F32) * s).astype(BF16)


def _sgu(gv, ug, w_s, b_s, ln_g, ln_b, batch, rows, keep_last_only):
    m, db = gv.shape
    groups = w_s.shape[0]
    n_chunks = m // (batch * rows)
    per_step = math.gcd(n_chunks, SGU_CHUNKS_PER_STEP) if keep_last_only else 1
    n_steps = n_chunks // per_step
    kdim = max(rows, LANES)
    w = w_s[:, :rows, :rows]
    if kdim > rows:
        w = jnp.pad(w, ((0, 0), (0, 0), (0, kdim - rows)))
    bs_t = b_s[:, :rows].T
    row_map = lambda b, c: (b * n_steps + c, 0)
    full2 = lambda b, c: (0, 0)
    if keep_last_only:
        vn_shape, vn_spec = (batch, rows, db), pl.BlockSpec((1, rows, db), lambda b, c: (b, 0, 0))
    else:
        vn_shape = (batch * n_chunks, rows, db)
        vn_spec = pl.BlockSpec((1, rows, db), lambda b, c: (b * n_steps + c, 0, 0))
    step_rows = per_step * rows
    return pl.pallas_call(
        functools.partial(_sgu_kernel, rows=rows, gdim=db // groups),
        grid=(batch, n_steps),
        in_specs=[pl.BlockSpec((step_rows, db), row_map),
                  pl.BlockSpec((step_rows, db), row_map),
                  pl.BlockSpec((groups, rows, kdim), lambda b, c: (0, 0, 0)),
                  pl.BlockSpec((rows, groups), full2),
                  pl.BlockSpec((1, db), full2),
                  pl.BlockSpec((1, db), full2)],
        out_specs=[pl.BlockSpec((step_rows, db), row_map), vn_spec],
        out_shape=[jax.ShapeDtypeStruct((m, db), BF16), jax.ShapeDtypeStruct(vn_shape, F32)],
        compiler_params=_params(2),
        name="sgu",
    )(gv, ug, w, bs_t, ln_g.reshape(1, db), ln_b.reshape(1, db))


def _bias_kernel(rb_ref, t_ref):
    wide = KEY_WIN + Q_BLOCK
    n_tab = rb_ref.shape[1]
    pos = lax.broadcasted_iota(jnp.int32, (n_tab, wide), 1)
    tab = lax.broadcasted_iota(jnp.int32, (n_tab, wide), 0)
    idx = jnp.clip(KEY_WIN - pos, -REL_CLIP, REL_CLIP) + REL_CLIP
    base = jnp.sum(jnp.where(tab == idx, rb_ref[0], 0.0), axis=0, keepdims=True)
    rolled = pltpu.roll(jnp.broadcast_to(base, (Q_BLOCK, wide)), 0, 1, stride=1, stride_axis=0)
    t = rolled[:, Q_BLOCK:]
    qc = lax.broadcasted_iota(jnp.int32, (Q_BLOCK, KEY_WIN), 0) // CHUNK
    kc = lax.broadcasted_iota(jnp.int32, (Q_BLOCK, KEY_WIN), 1) // CHUNK
    t_ref[0] = jnp.where((qc <= kc) & (kc <= qc + PAST_CHUNKS), t, NEG_INF)


def _bias_table(rel_bias):
    heads, n_rel = rel_bias.shape
    n_tab = -(-n_rel // 8) * 8
    rb = jnp.pad(rel_bias, ((0, 0), (0, n_tab - n_rel))).reshape(heads, n_tab, 1)
    return pl.pallas_call(
        _bias_kernel,
        grid=(heads,),
        in_specs=[pl.BlockSpec((1, n_tab, 1), lambda h: (h, 0, 0))],
        out_specs=pl.BlockSpec((1, Q_BLOCK, KEY_WIN), lambda h: (h, 0, 0)),
        out_shape=jax.ShapeDtypeStruct((heads, Q_BLOCK, KEY_WIN), F32),
        compiler_params=_params(1),
        name="attn_bias_table",
    )(rb)


def _softmax_pv(s, v):
    m = jnp.max(s, axis=-1, keepdims=True)
    p = jnp.exp(s - m)
    l = jnp.sum(p, axis=-1, keepdims=True)
    return jnp.dot(p.astype(BF16), v, preferred_element_type=F32) / l


_NT = (((1,), (1,)), ((), ()))


def _attn_prompt_kernel(q_ref, k_ref, v_ref, zb_ref, t_ref, y_ref, nk_ref, nv_ref, kb_s, vb_s,
                        *, seq, keep, dh):
    kb_s[...] = k_ref[0].astype(BF16)
    vb_s[...] = v_ref[0].astype(BF16)
    nk_ref[0] = k_ref[0, seq - keep:, :]
    nv_ref[0] = v_ref[0, seq - keep:, :]

    for h in range(q_ref.shape[2] // dh):
        cols = slice(h * dh, (h + 1) * dh)
        for qs in range(0, seq, Q_BLOCK):
            ks = max(0, qs - WINDOW)
            rows, keys = slice(qs, qs + Q_BLOCK), slice(ks, qs + Q_BLOCK)
            bias = t_ref[h, :, KEY_WIN - (keys.stop - keys.start):]
            s = lax.dot_general(q_ref[0, rows, cols], kb_s[keys, cols], _NT,
                                preferred_element_type=F32) + bias
            o = _softmax_pv(s, vb_s[keys, cols])
            y_ref[0, rows, cols] = (o * zb_ref[0, rows, cols].astype(F32)).astype(BF16)


ATTN_HEADS_PER_STEP = 2


def _attn_prompt(q, kv, zb, table, batch, seq):
    db = q.shape[1]
    dh = db // N_HEADS
    keep = min(WINDOW, seq)
    assert seq % Q_BLOCK == 0 and dh % LANES == 0
    shape3 = (batch, seq, db)
    kv3 = kv.reshape(batch, seq, 2 * db)
    hps = math.gcd(N_HEADS, ATTN_HEADS_PER_STEP)
    n_steps = N_HEADS // hps
    head = pl.BlockSpec((1, seq, hps * dh), lambda b, h: (b, 0, h))
    v_head = pl.BlockSpec((1, seq, hps * dh), lambda b, h: (b, 0, n_steps + h))
    kept = pl.BlockSpec((1, keep, hps * dh), lambda b, h: (b, 0, h))
    return pl.pallas_call(
        functools.partial(_attn_prompt_kernel, seq=seq, keep=keep, dh=dh),
        grid=(batch, n_steps),
        in_specs=[head, head, v_head, head,
                  pl.BlockSpec((hps, Q_BLOCK, KEY_WIN), lambda b, h: (h, 0, 0))],
        out_specs=[head, kept, kept],
        out_shape=[jax.ShapeDtypeStruct(shape3, BF16),
                   jax.ShapeDtypeStruct((batch, keep, db), F32),
                   jax.ShapeDtypeStruct((batch, keep, db), F32)],
        scratch_shapes=[pltpu.VMEM((seq, hps * dh), BF16), pltpu.VMEM((seq, hps * dh), BF16)],
        compiler_params=_params(2),
        name="attn_prompt",
    )(q.reshape(shape3), kv3, kv3, zb.reshape(shape3), table)


HEAD_GROUP = 8


def _attn_sample_kernel(q_ref, kn_ref, vn_ref, zb_ref, kc_ref, vc_ref, b_ref, y_ref):
    n_new, dh = q_ref.shape[1], q_ref.shape[-1]
    rows_cache = kc_ref.shape[2] * HEAD_GROUP
    rows_new = n_new * HEAD_GROUP
    for g in range(q_ref.shape[2] // HEAD_GROUP):
        heads = slice(g * HEAD_GROUP, (g + 1) * HEAD_GROUP)
        flat = lambda ref, *lead: ref[(*lead, slice(None), heads, slice(None))].reshape(-1, dh)
        q = flat(q_ref, 0).astype(BF16)
        s_c = lax.dot_general(q, flat(kc_ref, 0, 0).astype(BF16), _NT,
                              preferred_element_type=F32) + b_ref[g, :, :rows_cache]
        s_n = lax.dot_general(q, flat(kn_ref, 0).astype(BF16), _NT,
                              preferred_element_type=F32) + b_ref[g, :, rows_cache:rows_cache + rows_new]
        m = jnp.maximum(jnp.max(s_c, axis=-1, keepdims=True), jnp.max(s_n, axis=-1, keepdims=True))
        p_c = jnp.exp(s_c - m)
        p_n = jnp.exp(s_n - m)
        l = jnp.sum(p_c, axis=-1, keepdims=True) + jnp.sum(p_n, axis=-1, keepdims=True)
        o = (jnp.dot(p_c.astype(BF16), flat(vc_ref, 0, 0).astype(BF16), preferred_element_type=F32)
             + jnp.dot(p_n.astype(BF16), flat(vn_ref, 0).astype(BF16), preferred_element_type=F32))
        y_ref[0, :, heads, :] = (o / l * flat(zb_ref, 0)).reshape(n_new, HEAD_GROUP, dh)


def _sample_bias_kernel(a_ref, b_ref):
    rows = a_ref.shape[1]
    pos = lax.broadcasted_iota(jnp.int32, (LANES, LANES * HEAD_GROUP), 0)
    col = lax.broadcasted_iota(jnp.int32, (LANES, LANES * HEAD_GROUP), 1)
    expand = (col // HEAD_GROUP == pos).astype(F32)
    r = lax.broadcasted_iota(jnp.int32, (rows, LANES * HEAD_GROUP), 0)
    c = lax.broadcasted_iota(jnp.int32, (rows, LANES * HEAD_GROUP), 1)
    same_head = (r % HEAD_GROUP) == (c % HEAD_GROUP)
    for blk in range(a_ref.shape[2] // LANES):
        wide = jnp.dot(a_ref[0, :, blk * LANES:(blk + 1) * LANES], expand,
                       preferred_element_type=F32, precision=lax.Precision.HIGHEST)
        b_ref[0, :, blk * LANES * HEAD_GROUP:(blk + 1) * LANES * HEAD_GROUP] = jnp.where(
            same_head, wide, NEG_INF)


def _sample_bias(table, n_new, n_pos):
    groups = N_HEADS // HEAD_GROUP
    rows = n_new * HEAD_GROUP
    n_pad = -(-n_pos // LANES) * LANES
    a = table[:, :n_new, :n_pos].reshape(groups, HEAD_GROUP, n_new, n_pos).transpose(0, 2, 1, 3)
    a = jnp.pad(a.reshape(groups, rows, n_pos), ((0, 0), (0, 0), (0, n_pad - n_pos)))
    return pl.pallas_call(
        _sample_bias_kernel,
        grid=(groups,),
        in_specs=[pl.BlockSpec((1, rows, n_pad), lambda g: (g, 0, 0))],
        out_specs=pl.BlockSpec((1, rows, n_pad * HEAD_GROUP), lambda g: (g, 0, 0)),
        out_shape=jax.ShapeDtypeStruct((groups, rows, n_pad * HEAD_GROUP), F32),
        compiler_params=_params(1),
        name="attn_sample_bias",
    )(a)


def _attn_sample(q, kv, zb, k_cache, v_cache, layer, table, batch, n_new):
    db = q.shape[1]
    dh = db // N_HEADS
    n_cache = k_cache.shape[2]
    assert n_cache == WINDOW and n_new <= CHUNK and N_HEADS % HEAD_GROUP == 0
    bias = _sample_bias(table, n_new, n_cache + n_new)

    shape4 = (batch, n_new, N_HEADS, dh)
    new = pl.BlockSpec((1, n_new, N_HEADS, dh), lambda b: (b, 0, 0, 0))
    new_v = pl.BlockSpec((1, n_new, N_HEADS, dh), lambda b: (b, 0, 1, 0))
    cache = pl.BlockSpec((1, 1, n_cache, N_HEADS, dh), lambda b: (layer, b, 0, 0, 0))
    kv4 = kv.reshape(batch, n_new, 2 * N_HEADS, dh)
    y = pl.pallas_call(
        _attn_sample_kernel,
        grid=(batch,),
        in_specs=[new, new, new_v, new, cache, cache,
                  pl.BlockSpec(bias.shape, lambda b: (0, 0, 0))],
        out_specs=new,
        out_shape=jax.ShapeDtypeStruct(shape4, F32),
        compiler_params=_params(1),
        name="attn_sample",
    )(q.astype(F32).reshape(shape4), kv4, kv4, zb.astype(F32).reshape(shape4), k_cache, v_cache,
      bias)
    return y.reshape(batch * n_new, db).astype(BF16)


def _merge_kernel(ya_ref, yb_ref, wa_ref, wb_ref, ga_ref, gb_ref, m_ref):
    a = jnp.dot(ya_ref[...], wa_ref[...], preferred_element_type=F32)
    b = jnp.dot(yb_ref[...], wb_ref[...], preferred_element_type=F32)
    m_ref[...] = (ga_ref[...].astype(F32) * a + gb_ref[...].astype(F32) * b).astype(BF16)


def _merge(ya, yb, w_up, gates):
    m, kb = ya.shape
    n = w_up.shape[1] // 2
    tm, tn = _tile(m, MATMUL_TILE), _tile(n, MATMUL_TILE)
    nj = n // tn
    lhs = pl.BlockSpec((tm, kb), lambda i, j: (i, 0))
    rhs_a = pl.BlockSpec((kb, tn), lambda i, j: (0, j))
    rhs_b = pl.BlockSpec((kb, tn), lambda i, j: (0, nj + j))
    return pl.pallas_call(
        _merge_kernel,
        grid=(m // tm, nj),
        in_specs=[lhs, lhs, rhs_a, rhs_b,
                  pl.BlockSpec((tm, tn), lambda i, j: (i, j)),
                  pl.BlockSpec((tm, tn), lambda i, j: (i, nj + j))],
        out_specs=pl.BlockSpec((tm, tn), lambda i, j: (i, j)),
        out_shape=jax.ShapeDtypeStruct((m, n), BF16),
        compiler_params=_params(2),
        name="merge",
    )(ya, yb, w_up, w_up, gates, gates)


def _tail_kernel(m_ref, w_ref, p_ref, wp_ref, g_ref, x_hbm, y_ref, o_s, x1b_s, ssq_s, xbuf, sem,
                 *, nj, tm, tn):
    i = pl.program_id(0)
    j = pl.program_id(1)
    x_copy = pltpu.make_async_copy(x_hbm.at[pl.ds(i * tm, tm), :], xbuf, sem)

    @pl.when(j == 0)
    def _():
        x_copy.start()
        ssq_s[...] = jnp.zeros_like(ssq_s)

    @pl.when(j < nj)
    def _():
        t = jnp.dot(m_ref[...], w_ref[...], preferred_element_type=F32)
        o_s[j] = t
        ssq_s[...] += jnp.sum(t * t, axis=-1, keepdims=True)

    def finish(gate_logits, x1_tile):
        proj = jnp.dot(p_ref[...].astype(BF16), wp_ref[...], preferred_element_type=F32)
        y_ref[...] = x1_tile + jax.nn.sigmoid(gate_logits) * proj

    @pl.when(j == nj)
    def _():
        x_copy.wait()
        rstd = lax.rsqrt(ssq_s[...] / (nj * tn) + EPS)
        logits = None
        for jb in range(nj):
            cols = slice(jb * tn, (jb + 1) * tn)
            x1 = xbuf[:, cols] + o_s[jb] * rstd * g_ref[:, cols]
            o_s[jb] = x1
            x1b = x1.astype(BF16)
            x1b_s[:, cols] = x1b
            part = jnp.dot(x1b, w_ref[cols, :], preferred_element_type=F32)
            logits = part if logits is None else logits + part
        finish(logits, o_s[0])

    @pl.when(j > nj)
    def _():
        finish(jnp.dot(x1b_s[...], w_ref[...], preferred_element_type=F32), o_s[j - nj])


def _tail(m_act, x, p, w_cat, w_pp, g):
    m, d = x.shape
    pd = p.shape[1]
    tm, tn = _tile(m, TAIL_ROWS), _tile(d, MATMUL_TILE)
    nj = d // tn
    second = lambda i, j: (0, jnp.maximum(j - nj, 0))
    return pl.pallas_call(
        functools.partial(_tail_kernel, nj=nj, tm=tm, tn=tn),
        grid=(m // tm, 2 * nj),
        in_specs=[pl.BlockSpec((tm, d), lambda i, j: (i, 0)),
                  pl.BlockSpec((d, tn), lambda i, j: (0, j)),
                  pl.BlockSpec((tm, pd), lambda i, j: (i, 0)),
                  pl.BlockSpec((pd, tn), second),
                  pl.BlockSpec((1, d), lambda i, j: (0, 0)),
                  pl.BlockSpec(memory_space=pl.ANY)],
        out_specs=pl.BlockSpec((tm, tn), lambda i, j: (i, jnp.maximum(j - nj, 0))),
        out_shape=jax.ShapeDtypeStruct((m, d), F32),
        scratch_shapes=[pltpu.VMEM((nj, tm, tn), F32),
                        pltpu.VMEM((tm, d), BF16),
                        pltpu.VMEM((tm, 1), F32),
                        pltpu.VMEM((tm, d), F32),
                        pltpu.SemaphoreType.DMA(())],
        compiler_params=_params(2),
        name="tail",
    )(m_act, w_cat, p, w_pp, g.reshape(1, d), x)


def _layer_common(x, pre_g, wb, pending):
    d = x.shape[1]
    db = d // 2
    h = _rmsnorm_bf16(x, pre_g)
    scale = (db // N_HEADS) ** -0.5

    def run(wname, offs, n_cols, epilogue, out_dtype, name, then=(), tn_pref=MATMUL_TILE):
        todo = [n for n in then if n in pending]
        outs = _project(h, wb[wname], offs, n_cols, epilogue, out_dtype, name, tn_pref,
                        side=[pending.pop(n) for n in todo])
        if not todo:
            return outs
        wb.update(zip(todo, outs[1:]))
        return outs[0]

    ug = run("gate", (0, db), db, _gmlp_gate, BF16, "proj_gmlp_gate", then=("v",),
             tn_pref=TWO_WEIGHT_COLS)
    gv = run("v", (0,), db, _gelu, F32, "proj_gmlp_v", then=("q",))
    q = run("q", (0,), db, lambda a: a * scale, BF16, "proj_q", then=("kv", "zb"))
    kv = run("kv", (0,), 2 * db, lambda a: a, F32, "proj_kv", then=("gates",))
    zb = run("zb", (0,), db, jax.nn.silu, BF16, "proj_zb")
    gates = run("gates", (0,), 2 * d, jax.nn.sigmoid, BF16, "proj_merge_gates", then=("up", "tail"))
    return ug, gv, q, kv, zb, gates


def _layer_tail(x, p, ya, yb, gates, wb, post_g, w_pp):
    m = _merge(ya, yb, wb["up"], gates)
    return _tail(m, x, p, wb["tail"], w_pp, post_g)


def kernel(x_prompt, x_sample, cache_attn_k, cache_attn_v, p_prompt, p_sample, norm_pre_g, norm_post_g, w_in, gmlp_ln_g, gmlp_ln_b, gmlp_w_s, gmlp_b_s, attn_rel_bias, w_up_a, w_up_b, w_out, w_ple_gate, w_ple_proj):
    batch, seq, d = x_prompt.shape
    dec_batch, dec_seq, _ = x_sample.shape
    db = d // 2
    dh = db // N_HEADS
    depth = w_in.shape[0]
    assert seq % GMLP_CHUNK == 0 and dec_seq <= CHUNK

    xp = x_prompt.reshape(batch * seq, d)
    xs = x_sample.reshape(dec_batch * dec_seq, d)
    outs = [[] for _ in range(6)]
    for i in range(depth):
        wi = w_in[i]
        seg = lambda first, n=1: (wi, first * db, n * db)
        wb = {"gate": _convert_bf16([seg(0), seg(2)])}
        pending = {"v": [seg(1)], "q": [seg(3)], "kv": [seg(4, 2)], "zb": [seg(6)],
                   "gates": [seg(7), seg(8), seg(9), seg(10)],
                   "up": [(w_up_a[i], 0, d), (w_up_b[i], 0, d)],
                   "tail": [(w_out[i], 0, d), (w_ple_gate[i], 0, d)]}
        w_pp = w_ple_proj[i].astype(BF16)
        table = _bias_table(attn_rel_bias[i])
        sgu_w = (gmlp_w_s[i], gmlp_b_s[i], gmlp_ln_g[i], gmlp_ln_b[i])

        ug, gv, q, kv, zb, gates = _layer_common(xp, norm_pre_g[i], wb, pending)
        ya, gp = _sgu(gv, ug, *sgu_w, batch=batch, rows=GMLP_CHUNK, keep_last_only=True)
        yb, kp, vp = _attn_prompt(q, kv, zb, table, batch, seq)
        xp = _layer_tail(xp, p_prompt[i].reshape(batch * seq, -1), ya, yb.reshape(batch * seq, db),
                         gates, wb, norm_post_g[i], w_pp)

        ug, gv, q, kv, zb, gates = _layer_common(xs, norm_pre_g[i], wb, pending)
        ya, gs = _sgu(gv, ug, *sgu_w, batch=dec_batch, rows=dec_seq, keep_last_only=False)
        yb = _attn_sample(q, kv, zb, cache_attn_k, cache_attn_v, i, table, dec_batch, dec_seq)
        xs = _layer_tail(xs, p_sample[i].reshape(dec_batch * dec_seq, -1), ya, yb, gates, wb,
                         norm_post_g[i], w_pp)

        keep = kp.shape[1]
        for lst, val in zip(outs, (kp.reshape(batch, keep, N_HEADS, dh),
                                   vp.reshape(batch, keep, N_HEADS, dh),
                                   kv[:, :db].reshape(dec_batch, dec_seq, N_HEADS, dh),
                                   kv[:, db:].reshape(dec_batch, dec_seq, N_HEADS, dh),
                                   gp, gs)):
            lst.append(val)
    return (xp.reshape(batch, seq, d), xs.reshape(dec_batch, dec_seq, d),
            *[jnp.stack(lst) for lst in outs])
```

```python
import functools
import math

import jax
import jax.numpy as jnp
from jax import lax
from jax.experimental import pallas as pl
from jax.experimental.pallas import tpu as pltpu

CHUNK = 64
GMLP_CHUNK = 128
GMLP_GROUPS = 8
N_HEADS = 16
PAST_CHUNKS = 8
WINDOW = PAST_CHUNKS * CHUNK
REL_CLIP = 128
EPS = 1e-6
NEG_INF = -1e30

LANES = 128
Q_BLOCK = 256
KEY_WIN = WINDOW + Q_BLOCK
VMEM_LIMIT_BYTES = 56 * 1024 * 1024
MATMUL_TILE = 1024
TWO_WEIGHT_COLS = 512
TAIL_ROWS = 512
ROWWISE_ROWS = 512

F32 = jnp.float32
BF16 = jnp.bfloat16


def _params(n_axes):
    return pltpu.CompilerParams(dimension_semantics=("arbitrary",) * n_axes,
                                vmem_limit_bytes=VMEM_LIMIT_BYTES)


def _tile(n, pref):
    if n <= pref:
        return n
    t = (pref // LANES) * LANES
    while n % t:
        t -= LANES
    return t


def _rmsnorm_kernel(x_ref, g_ref, o_ref):
    x = x_ref[...]
    ms = jnp.mean(x * x, axis=-1, keepdims=True)
    o_ref[...] = (x * lax.rsqrt(ms + EPS) * g_ref[...]).astype(o_ref.dtype)


def _rmsnorm_bf16(x, g):
    m, d = x.shape
    tr = _tile(m, ROWWISE_ROWS)
    return pl.pallas_call(
        _rmsnorm_kernel,
        grid=(m // tr,),
        in_specs=[pl.BlockSpec((tr, d), lambda i: (i, 0)),
                  pl.BlockSpec((1, d), lambda i: (0, 0))],
        out_specs=pl.BlockSpec((tr, d), lambda i: (i, 0)),
        out_shape=jax.ShapeDtypeStruct((m, d), BF16),
        compiler_params=_params(1),
        name="pre_rmsnorm",
    )(x, g.reshape(1, d))


def _proj_kernel(*refs, n_w, epilogue, side_layout):
    h = refs[0][...]
    accs = [jnp.dot(h, w[...], preferred_element_type=F32) for w in refs[1:1 + n_w]]
    n_side_in = sum(len(widths) for widths in side_layout)
    side_in = refs[1 + n_w:1 + n_w + n_side_in]
    o_ref = refs[1 + n_w + n_side_in]
    side_out = refs[2 + n_w + n_side_in:]
    o_ref[...] = epilogue(*accs).astype(o_ref.dtype)
    k = 0
    for out_ref, widths in zip(side_out, side_layout):
        off = 0
        for width in widths:
            out_ref[:, off:off + width] = side_in[k][...].astype(BF16)
            off += width
            k += 1


def _col_map(i, j, *, first):
    return (0, first + j)


def _slab_map(i, j, *, nj, col):
    return (i * nj + j, col)


def _project(h, w, col_offsets, n_cols, epilogue, out_dtype, name, tn_pref=MATMUL_TILE, side=()):
    m, k = h.shape
    tm = _tile(m, MATMUL_TILE)
    tn = _tile(math.gcd(n_cols, *col_offsets), tn_pref)
    n_w = len(col_offsets)
    ni, nj = m // tm, n_cols // tn
    w_specs = [pl.BlockSpec((k, tn), functools.partial(_col_map, first=off // tn))
               for off in col_offsets]
    side_in, side_specs, side_out_specs, side_shapes, layout = [], [], [], [], []
    for group in side:
        rows = group[0][0].shape[0]
        slab = rows // (ni * nj)
        assert slab * ni * nj == rows and slab % 16 == 0
        total = sum(width for _, _, width in group)
        for src, col0, width in group:
            assert col0 % width == 0
            side_in.append(src)
            side_specs.append(pl.BlockSpec((slab, width),
                                           functools.partial(_slab_map, nj=nj, col=col0 // width)))
        side_out_specs.append(pl.BlockSpec((slab, total), functools.partial(_slab_map, nj=nj, col=0)))
        side_shapes.append(jax.ShapeDtypeStruct((rows, total), BF16))
        layout.append(tuple(width for _, _, width in group))
    outs = pl.pallas_call(
        functools.partial(_proj_kernel, n_w=n_w, epilogue=epilogue, side_layout=tuple(layout)),
        grid=(ni, nj),
        in_specs=[pl.BlockSpec((tm, k), lambda i, j: (i, 0))] + w_specs + side_specs,
        out_specs=[pl.BlockSpec((tm, tn), lambda i, j: (i, j))] + side_out_specs,
        out_shape=[jax.ShapeDtypeStruct((m, n_cols), out_dtype)] + side_shapes,
        compiler_params=_params(2),
        name=name,
    )(h, *([w] * n_w), *side_in)
    return outs[0] if not side else outs


def _convert_kernel(*refs):
    *pieces, out_ref = refs
    off = 0
    for piece in pieces:
        out_ref[:, off:off + piece.shape[1]] = piece[...].astype(BF16)
        off += piece.shape[1]


def _convert_bf16(group):
    rows = group[0][0].shape[0]
    slab = _tile(rows, ROWWISE_ROWS)
    total = sum(width for _, _, width in group)
    return pl.pallas_call(
        _convert_kernel,
        grid=(rows // slab,),
        in_specs=[pl.BlockSpec((slab, width), functools.partial(lambda i, col: (i, col), col=col0 // width))
                  for _, col0, width in group],
        out_specs=pl.BlockSpec((slab, total), lambda i: (i, 0)),
        out_shape=jax.ShapeDtypeStruct((rows, total), BF16),
        compiler_params=_params(1),
        name="convert_weights",
    )(*[src for src, _, _ in group])


def _gelu(x):
    return 0.5 * x * (1.0 + lax.erf(x * math.sqrt(0.5)))


def _gmlp_gate(u, z):
    return _gelu(u) * jax.nn.silu(z)


SGU_CHUNKS_PER_STEP = 8


def _sgu_kernel(gv_ref, ug_ref, w_ref, bs_ref, lng_ref, lnb_ref, ya_ref, vn_ref, *, rows, gdim):
    kdim = w_ref.shape[2]
    i = lax.broadcasted_iota(jnp.int32, (rows, kdim), 0)
    j = lax.broadcasted_iota(jnp.int32, (rows, kdim), 1)
    visible = (j // CHUNK) <= (i // CHUNK)
    w = [jnp.where(visible, w_ref[g], 0.0).astype(BF16) for g in range(w_ref.shape[0])]
    for r0 in range(0, gv_ref.shape[0], rows):
        chunk = slice(r0, r0 + rows)
        x = gv_ref[chunk, :]
        xc = x - jnp.mean(x, axis=-1, keepdims=True)
        var = jnp.mean(xc * xc, axis=-1, keepdims=True)
        vn = xc * lax.rsqrt(var + EPS) * lng_ref[...] + lnb_ref[...]
        if r0 + rows == gv_ref.shape[0]:
            vn_ref[0] = vn
        vnb = vn.astype(BF16)
        if kdim > rows:
            vnb = jnp.concatenate([vnb, jnp.zeros((kdim - rows, vnb.shape[1]), BF16)], axis=0)
        for g, wg in enumerate(w):
            sl = slice(g * gdim, (g + 1) * gdim)
            s = jnp.dot(wg, vnb[:, sl], preferred_element_type=F32) + bs_ref[:, g:g + 1]
            ya_ref[chunk, sl] = (ug_ref[chunk, sl].astype(F32) * s).astype(BF16)


def _sgu(gv, ug, w_s, b_s, ln_g, ln_b, batch, rows, keep_last_only):
    m, db = gv.shape
    groups = w_s.shape[0]
    n_chunks = m // (batch * rows)
    per_step = math.gcd(n_chunks, SGU_CHUNKS_PER_STEP) if keep_last_only else 1
    n_steps = n_chunks // per_step
    kdim = max(rows, LANES)
    w = w_s[:, :rows, :rows]
    if kdim > rows:
        w = jnp.pad(w, ((0, 0), (0, 0), (0, kdim - rows)))
    bs_t = b_s[:, :rows].T
    row_map = lambda b, c: (b * n_steps + c, 0)
    full2 = lambda b, c: (0, 0)
    if keep_last_only:
        vn_shape, vn_spec = (batch, rows, db), pl.BlockSpec((1, rows, db), lambda b, c: (b, 0, 0))
    else:
        vn_shape = (batch * n_chunks, rows, db)
        vn_spec = pl.BlockSpec((1, rows, db), lambda b, c: (b * n_steps + c, 0, 0))
    step_rows = per_step * rows
    return pl.pallas_call(
        functools.partial(_sgu_kernel, rows=rows, gdim=db // groups),
        grid=(batch, n_steps),
        in_specs=[pl.BlockSpec((step_rows, db), row_map),
                  pl.BlockSpec((step_rows, db), row_map),
                  pl.BlockSpec((groups, rows, kdim), lambda b, c: (0, 0, 0)),
                  pl.BlockSpec((rows, groups), full2),
                  pl.BlockSpec((1, db), full2),
                  pl.BlockSpec((1, db), full2)],
        out_specs=[pl.BlockSpec((step_rows, db), row_map), vn_spec],
        out_shape=[jax.ShapeDtypeStruct((m, db), BF16), jax.ShapeDtypeStruct(vn_shape, F32)],
        compiler_params=_params(2),
        name="sgu",
    )(gv, ug, w, bs_t, ln_g.reshape(1, db), ln_b.reshape(1, db))


def _bias_kernel(rb_ref, t_ref):
    wide = KEY_WIN + Q_BLOCK
    n_tab = rb_ref.shape[1]
    pos = lax.broadcasted_iota(jnp.int32, (n_tab, wide), 1)
    tab = lax.broadcasted_iota(jnp.int32, (n_tab, wide), 0)
    idx = jnp.clip(KEY_WIN - pos, -REL_CLIP, REL_CLIP) + REL_CLIP
    base = jnp.sum(jnp.where(tab == idx, rb_ref[0], 0.0), axis=0, keepdims=True)
    rolled = pltpu.roll(jnp.broadcast_to(base, (Q_BLOCK, wide)), 0, 1, stride=1, stride_axis=0)
    t = rolled[:, Q_BLOCK:]
    qc = lax.broadcasted_iota(jnp.int32, (Q_BLOCK, KEY_WIN), 0) // CHUNK
    kc = lax.broadcasted_iota(jnp.int32, (Q_BLOCK, KEY_WIN), 1) // CHUNK
    t_ref[0] = jnp.where((qc <= kc) & (kc <= qc + PAST_CHUNKS), t, NEG_INF)


def _bias_table(rel_bias):
    heads, n_rel = rel_bias.shape
    n_tab = -(-n_rel // 8) * 8
    rb = jnp.pad(rel_bias, ((0, 0), (0, n_tab - n_rel))).reshape(heads, n_tab, 1)
    return pl.pallas_call(
        _bias_kernel,
        grid=(heads,),
        in_specs=[pl.BlockSpec((1, n_tab, 1), lambda h: (h, 0, 0))],
        out_specs=pl.BlockSpec((1, Q_BLOCK, KEY_WIN), lambda h: (h, 0, 0)),
        out_shape=jax.ShapeDtypeStruct((heads, Q_BLOCK, KEY_WIN), F32),
        compiler_params=_params(1),
        name="attn_bias_table",
    )(rb)


def _softmax_pv(s, v):
    m = jnp.max(s, axis=-1, keepdims=True)
    p = jnp.exp(s - m)
    l = jnp.sum(p, axis=-1, keepdims=True)
    return jnp.dot(p.astype(BF16), v, preferred_element_type=F32) / l


_NT = (((1,), (1,)), ((), ()))


def _attn_prompt_kernel(q_ref, k_ref, v_ref, zb_ref, t_ref, y_ref, nk_ref, nv_ref, kb_s, vb_s,
                        *, seq, keep, dh):
    kb_s[...] = k_ref[0].astype(BF16)
    vb_s[...] = v_ref[0].astype(BF16)
    nk_ref[0] = k_ref[0, seq - keep:, :]
    nv_ref[0] = v_ref[0, seq - keep:, :]

    for h in range(q_ref.shape[2] // dh):
        cols = slice(h * dh, (h + 1) * dh)
        for qs in range(0, seq, Q_BLOCK):
            ks = max(0, qs - WINDOW)
            rows, keys = slice(qs, qs + Q_BLOCK), slice(ks, qs + Q_BLOCK)
            bias = t_ref[h, :, KEY_WIN - (keys.stop - keys.start):]
            s = lax.dot_general(q_ref[0, rows, cols], kb_s[keys, cols], _NT,
                                preferred_element_type=F32) + bias
            o = _softmax_pv(s, vb_s[keys, cols])
            y_ref[0, rows, cols] = (o * zb_ref[0, rows, cols].astype(F32)).astype(BF16)


ATTN_HEADS_PER_STEP = 2


def _attn_prompt(q, kv, zb, table, batch, seq):
    db = q.shape[1]
    dh = db // N_HEADS
    keep = min(WINDOW, seq)
    assert seq % Q_BLOCK == 0 and dh % LANES == 0
    shape3 = (batch, seq, db)
    kv3 = kv.reshape(batch, seq, 2 * db)
    hps = math.gcd(N_HEADS, ATTN_HEADS_PER_STEP)
    n_steps = N_HEADS // hps
    head = pl.BlockSpec((1, seq, hps * dh), lambda b, h: (b, 0, h))
    v_head = pl.BlockSpec((1, seq, hps * dh), lambda b, h: (b, 0, n_steps + h))
    kept = pl.BlockSpec((1, keep, hps * dh), lambda b, h: (b, 0, h))
    return pl.pallas_call(
        functools.partial(_attn_prompt_kernel, seq=seq, keep=keep, dh=dh),
        grid=(batch, n_steps),
        in_specs=[head, head, v_head, head,
                  pl.BlockSpec((hps, Q_BLOCK, KEY_WIN), lambda b, h: (h, 0, 0))],
        out_specs=[head, kept, kept],
        out_shape=[jax.ShapeDtypeStruct(shape3, BF16),
                   jax.ShapeDtypeStruct((batch, keep, db), F32),
                   jax.ShapeDtypeStruct((batch, keep, db), F32)],
        scratch_shapes=[pltpu.VMEM((seq, hps * dh), BF16), pltpu.VMEM((seq, hps * dh), BF16)],
        compiler_params=_params(2),
        name="attn_prompt",
    )(q.reshape(shape3), kv3, kv3, zb.reshape(shape3), table)


HEAD_GROUP = 8


def _attn_sample_kernel(q_ref, kn_ref, vn_ref, zb_ref, kc_ref, vc_ref, b_ref, y_ref):
    n_new, dh = q_ref.shape[1], q_ref.shape[-1]
    rows_cache = kc_ref.shape[2] * HEAD_GROUP
    rows_new = n_new * HEAD_GROUP
    for g in range(q_ref.shape[2] // HEAD_GROUP):
        heads = slice(g * HEAD_GROUP, (g + 1) * HEAD_GROUP)
        flat = lambda ref, *lead: ref[(*lead, slice(None), heads, slice(None))].reshape(-1, dh)
        q = flat(q_ref, 0).astype(BF16)
        s_c = lax.dot_general(q, flat(kc_ref, 0, 0).astype(BF16), _NT,
                              preferred_element_type=F32) + b_ref[g, :, :rows_cache]
        s_n = lax.dot_general(q, flat(kn_ref, 0).astype(BF16), _NT,
                              preferred_element_type=F32) + b_ref[g, :, rows_cache:rows_cache + rows_new]
        m = jnp.maximum(jnp.max(s_c, axis=-1, keepdims=True), jnp.max(s_n, axis=-1, keepdims=True))
        p_c = jnp.exp(s_c - m)
        p_n = jnp.exp(s_n - m)
        l = jnp.sum(p_c, axis=-1, keepdims=True) + jnp.sum(p_n, axis=-1, keepdims=True)
        o = (jnp.dot(p_c.astype(BF16), flat(vc_ref, 0, 0).astype(BF16), preferred_element_type=F32)
             + jnp.dot(p_n.astype(BF16), flat(vn_ref, 0).astype(BF16), preferred_element_type=F32))
        y_ref[0, :, heads, :] = (o / l * flat(zb_ref, 0)).reshape(n_new, HEAD_GROUP, dh)


def _sample_bias_kernel(a_ref, b_ref):
    rows = a_ref.shape[1]
    pos = lax.broadcasted_iota(jnp.int32, (LANES, LANES * HEAD_GROUP), 0)
    col = lax.broadcasted_iota(jnp.int32, (LANES, LANES * HEAD_GROUP), 1)
    expand = (col // HEAD_GROUP == pos).astype(F32)
    r = lax.broadcasted_iota(jnp.int32, (rows, LANES * HEAD_GROUP), 0)
    c = lax.broadcasted_iota(jnp.int32, (rows, LANES * HEAD_GROUP), 1)
    same_head = (r % HEAD_GROUP) == (c % HEAD_GROUP)
    for blk in range(a_ref.shape[2] // LANES):
        wide = jnp.dot(a_ref[0, :, blk * LANES:(blk + 1) * LANES], expand,
                       preferred_element_type=F32, precision=lax.Precision.HIGHEST)
        b_ref[0, :, blk * LANES * HEAD_GROUP:(blk + 1) * LANES * HEAD_GROUP] = jnp.where(
            same_head, wide, NEG_INF)


def _sample_bias(table, n_new, n_pos):
    groups = N_HEADS // HEAD_GROUP
    rows = n_new * HEAD_GROUP
    n_pad = -(-n_pos // LANES) * LANES
    a = table[:, :n_new, :n_pos].reshape(groups, HEAD_GROUP, n_new, n_pos).transpose(0, 2, 1, 3)
    a = jnp.pad(a.reshape(groups, rows, n_pos), ((0, 0), (0, 0), (0, n_pad - n_pos)))
    return pl.pallas_call(
        _sample_bias_kernel,
        grid=(groups,),
        in_specs=[pl.BlockSpec((1, rows, n_pad), lambda g: (g, 0, 0))],
        out_specs=pl.BlockSpec((1, rows, n_pad * HEAD_GROUP), lambda g: (g, 0, 0)),
        out_shape=jax.ShapeDtypeStruct((groups, rows, n_pad * HEAD_GROUP), F32),
        compiler_params=_params(1),
        name="attn_sample_bias",
    )(a)


def _attn_sample(q, kv, zb, k_cache, v_cache, layer, table, batch, n_new):
    db = q.shape[1]
    dh = db // N_HEADS
    n_cache = k_cache.shape[2]
    assert n_cache == WINDOW and n_new <= CHUNK and N_HEADS % HEAD_GROUP == 0
    bias = _sample_bias(table, n_new, n_cache + n_new)

    shape4 = (batch, n_new, N_HEADS, dh)
    new = pl.BlockSpec((1, n_new, N_HEADS, dh), lambda b: (b, 0, 0, 0))
    new_v = pl.BlockSpec((1, n_new, N_HEADS, dh), lambda b: (b, 0, 1, 0))
    cache = pl.BlockSpec((1, 1, n_cache, N_HEADS, dh), lambda b: (layer, b, 0, 0, 0))
    kv4 = kv.reshape(batch, n_new, 2 * N_HEADS, dh)
    y = pl.pallas_call(
        _attn_sample_kernel,
        grid=(batch,),
        in_specs=[new, new, new_v, new, cache, cache,
                  pl.BlockSpec(bias.shape, lambda b: (0, 0, 0))],
        out_specs=new,
        out_shape=jax.ShapeDtypeStruct(shape4, F32),
        compiler_params=_params(1),
        name="attn_sample",
    )(q.astype(F32).reshape(shape4), kv4, kv4, zb.astype(F32).reshape(shape4), k_cache, v_cache,
      bias)
    return y.reshape(batch * n_new, db).astype(BF16)


def _merge_kernel(ya_ref, yb_ref, wa_ref, wb_ref, ga_ref, gb_ref, m_ref):
    a = jnp.dot(ya_ref[...], wa_ref[...], preferred_element_type=F32)
    b = jnp.dot(yb_ref[...], wb_ref[...], preferred_element_type=F32)
    m_ref[...] = (ga_ref[...].astype(F32) * a + gb_ref[...].astype(F32) * b).astype(BF16)


def _merge(ya, yb, w_up, gates):
    m, kb = ya.shape
    n = w_up.shape[1] // 2
    tm, tn = _tile(m, MATMUL_TILE), _tile(n, MATMUL_TILE)
    nj = n // tn
    lhs = pl.BlockSpec((tm, kb), lambda i, j: (i, 0))
    rhs_a = pl.BlockSpec((kb, tn), lambda i, j: (0, j))
    rhs_b = pl.BlockSpec((kb, tn), lambda i, j: (0, nj + j))
    return pl.pallas_call(
        _merge_kernel,
        grid=(m // tm, nj),
        in_specs=[lhs, lhs, rhs_a, rhs_b,
                  pl.BlockSpec((tm, tn), lambda i, j: (i, j)),
                  pl.BlockSpec((tm, tn), lambda i, j: (i, nj + j))],
        out_specs=pl.BlockSpec((tm, tn), lambda i, j: (i, j)),
        out_shape=jax.ShapeDtypeStruct((m, n), BF16),
        compiler_params=_params(2),
        name="merge",
    )(ya, yb, w_up, w_up, gates, gates)


def _tail_kernel(m_ref, w_ref, p_ref, wp_ref, g_ref, x_hbm, y_ref, o_s, x1b_s, ssq_s, xbuf, sem,
                 *, nj, tm, tn):
    i = pl.program_id(0)
    j = pl.program_id(1)
    x_copy = pltpu.make_async_copy(x_hbm.at[pl.ds(i * tm, tm), :], xbuf, sem)

    @pl.when(j == 0)
    def _():
        x_copy.start()
        ssq_s[...] = jnp.zeros_like(ssq_s)

    @pl.when(j < nj)
    def _():
        t = jnp.dot(m_ref[...], w_ref[...], preferred_element_type=F32)
        o_s[j] = t
        ssq_s[...] += jnp.sum(t * t, axis=-1, keepdims=True)

    def finish(gate_logits, x1_tile):
        proj = jnp.dot(p_ref[...].astype(BF16), wp_ref[...], preferred_element_type=F32)
        y_ref[...] = x1_tile + jax.nn.sigmoid(gate_logits) * proj

    @pl.when(j == nj)
    def _():
        x_copy.wait()
        rstd = lax.rsqrt(ssq_s[...] / (nj * tn) + EPS)
        logits = None
        for jb in range(nj):
            cols = slice(jb * tn, (jb + 1) * tn)
            x1 = xbuf[:, cols] + o_s[jb] * rstd * g_ref[:, cols]
            o_s[jb] = x1
            x1b = x1.astype(BF16)
            x1b_s[:, cols] = x1b
            part = jnp.dot(x1b, w_ref[cols, :], preferred_element_type=F32)
            logits = part if logits is None else logits + part
        finish(logits, o_s[0])

    @pl.when(j > nj)
    def _():
        finish(jnp.dot(x1b_s[...], w_ref[...], preferred_element_type=F32), o_s[j - nj])


def _tail(m_act, x, p, w_cat, w_pp, g):
    m, d = x.shape
    pd = p.shape[1]
    tm, tn = _tile(m, TAIL_ROWS), _tile(d, MATMUL_TILE)
    nj = d // tn
    second = lambda i, j: (0, jnp.maximum(j - nj, 0))
    return pl.pallas_call(
        functools.partial(_tail_kernel, nj=nj, tm=tm, tn=tn),
        grid=(m // tm, 2 * nj),
        in_specs=[pl.BlockSpec((tm, d), lambda i, j: (i, 0)),
                  pl.BlockSpec((d, tn), lambda i, j: (0, j)),
                  pl.BlockSpec((tm, pd), lambda i, j: (i, 0)),
                  pl.BlockSpec((pd, tn), second),
                  pl.BlockSpec((1, d), lambda i, j: (0, 0)),
                  pl.BlockSpec(memory_space=pl.ANY)],
        out_specs=pl.BlockSpec((tm, tn), lambda i, j: (i, jnp.maximum(j - nj, 0))),
        out_shape=jax.ShapeDtypeStruct((m, d), F32),
        scratch_shapes=[pltpu.VMEM((nj, tm, tn), F32),
                        pltpu.VMEM((tm, d), BF16),
                        pltpu.VMEM((tm, 1), F32),
                        pltpu.VMEM((tm, d), F32),
                        pltpu.SemaphoreType.DMA(())],
        compiler_params=_params(2),
        name="tail",
    )(m_act, w_cat, p, w_pp, g.reshape(1, d), x)


def _layer_common(x, pre_g, wb, pending):
    d = x.shape[1]
    db = d // 2
    h = _rmsnorm_bf16(x, pre_g)
    scale = (db // N_HEADS) ** -0.5

    def run(wname, offs, n_cols, epilogue, out_dtype, name, then=(), tn_pref=MATMUL_TILE):
        todo = [n for n in then if n in pending]
        outs = _project(h, wb[wname], offs, n_cols, epilogue, out_dtype, name, tn_pref,
                        side=[pending.pop(n) for n in todo])
        if not todo:
            return outs
        wb.update(zip(todo, outs[1:]))
        return outs[0]

    ug = run("gate", (0, db), db, _gmlp_gate, BF16, "proj_gmlp_gate", then=("v",),
             tn_pref=TWO_WEIGHT_COLS)
    gv = run("v", (0,), db, _gelu, F32, "proj_gmlp_v", then=("q",))
    q = run("q", (0,), db, lambda a: a * scale, BF16, "proj_q", then=("kv", "zb"))
    kv = run("kv", (0,), 2 * db, lambda a: a, F32, "proj_kv", then=("gates",))
    zb = run("zb", (0,), db, jax.nn.silu, BF16, "proj_zb")
    gates = run("gates", (0,), 2 * d, jax.nn.sigmoid, BF16, "proj_merge_gates", then=("up", "tail"))
    return ug, gv, q, kv, zb, gates


def _layer_tail(x, p, ya, yb, gates, wb, post_g, w_pp):
    m = _merge(ya, yb, wb["up"], gates)
    return _tail(m, x, p, wb["tail"], w_pp, post_g)


def kernel(x_prompt, x_sample, cache_attn_k, cache_attn_v, p_prompt, p_sample, norm_pre_g, norm_post_g, w_in, gmlp_ln_g, gmlp_ln_b, gmlp_w_s, gmlp_b_s, attn_rel_bias, w_up_a, w_up_b, w_out, w_ple_gate, w_ple_proj):
    batch, seq, d = x_prompt.shape
    dec_batch, dec_seq, _ = x_sample.shape
    db = d // 2
    dh = db // N_HEADS
    depth = w_in.shape[0]
    assert seq % GMLP_CHUNK == 0 and dec_seq <= CHUNK

    xp = x_prompt.reshape(batch * seq, d)
    xs = x_sample.reshape(dec_batch * dec_seq, d)
    outs = [[] for _ in range(6)]
    for i in range(depth):
        wi = w_in[i]
        seg = lambda first, n=1: (wi, first * db, n * db)
        wb = {"gate": _convert_bf16([seg(0), seg(2)])}
        pending = {"v": [seg(1)], "q": [seg(3)], "kv": [seg(4, 2)], "zb": [seg(6)],
                   "gates": [seg(7), seg(8), seg(9), seg(10)],
                   "up": [(w_up_a[i], 0, d), (w_up_b[i], 0, d)],
                   "tail": [(w_out[i], 0, d), (w_ple_gate[i], 0, d)]}
        w_pp = w_ple_proj[i].astype(BF16)
        table = _bias_table(attn_rel_bias[i])
        sgu_w = (gmlp_w_s[i], gmlp_b_s[i], gmlp_ln_g[i], gmlp_ln_b[i])

        ug, gv, q, kv, zb, gates = _layer_common(xp, norm_pre_g[i], wb, pending)
        ya, gp = _sgu(gv, ug, *sgu_w, batch=batch, rows=GMLP_CHUNK, keep_last_only=True)
        yb, kp, vp = _attn_prompt(q, kv, zb, table, batch, seq)
        xp = _layer_tail(xp, p_prompt[i].reshape(batch * seq, -1), ya, yb.reshape(batch * seq, db),
                         gates, wb, norm_post_g[i], w_pp)

        ug, gv, q, kv, zb, gates = _layer_common(xs, norm_pre_g[i], wb, pending)
        ya, gs = _sgu(gv, ug, *sgu_w, batch=dec_batch, rows=dec_seq, keep_last_only=False)
        yb = _attn_sample(q, kv, zb, cache_attn_k, cache_attn_v, i, table, dec_batch, dec_seq)
        xs = _layer_tail(xs, p_sample[i].reshape(dec_batch * dec_seq, -1), ya, yb, gates, wb,
                         norm_post_g[i], w_pp)

        keep = kp.shape[1]
        for lst, val in zip(outs, (kp.reshape(batch, keep, N_HEADS, dh),
                                   vp.reshape(batch, keep, N_HEADS, dh),
                                   kv[:, :db].reshape(dec_batch, dec_seq, N_HEADS, dh),
                                   kv[:, db:].reshape(dec_batch, dec_seq, N_HEADS, dh),
                                   gp, gs)):
            lst.append(val)
    return (xp.reshape(batch, seq, d), xs.reshape(dec_batch, dec_seq, d),
            *[jnp.stack(lst) for lst in outs])
```
